```python
import jax, jax.numpy as jnp
from jax import lax
import numpy as np

D_MODEL = 4096
BATCH = 2
SEQ = 8192
DEPTH = 4

CTX_LEN = 256
GRID_W = 64
N_MIXERS = 3
MIXER_FOURIER, MIXER_ATTN, MIXER_CONV = 0, 1, 2
N_FOURIER_LAYERS = (DEPTH + 2) // 3
N_ATTN_LAYERS = (DEPTH + 1) // 3
N_CONV_LAYERS = DEPTH // 3
N_SUB = 3
N_MOD = 3 * N_SUB
ADA_RANK = 1024
D_FF = 4096
FFN_RES_WEIGHT = 0.5
FOURIER_GROUPS = 4
FOURIER_GROUP_DIM = D_MODEL // FOURIER_GROUPS
HEAD_DIM = 128
N_Q_HEADS = D_MODEL // HEAD_DIM
N_KV_HEADS = 8
GQA_GROUP = N_Q_HEADS // N_KV_HEADS
Q_WIDTH = N_Q_HEADS * HEAD_DIM
KV_WIDTH = N_KV_HEADS * HEAD_DIM
QKV_WIDTH = Q_WIDTH + 2 * KV_WIDTH
Q_BLOCK = 128
ROPE_THETA = 10000.0
ROPE_AXIS_DIM = HEAD_DIM // 2
ROPE_PAIRS = ROPE_AXIS_DIM // 2
CONV_DIM = D_MODEL
CONV_WIDTH = 31
CONV_PAD = (CONV_WIDTH - 1) // 2
EPS = 1e-6

kernel_name = "hybrid_fnet_gqa_conformer_macaron_dit"


def rms_norm(x, w):
    xf = x.astype(jnp.float32)
    y = xf * lax.rsqrt(jnp.mean(xf * xf, axis=-1, keepdims=True) + EPS)
    return (y * w.astype(jnp.float32)).astype(x.dtype)


def layer_norm(x, w, b):
    xf = x.astype(jnp.float32)
    mu = jnp.mean(xf, axis=-1, keepdims=True)
    xc = xf - mu
    y = xc * lax.rsqrt(jnp.mean(xc * xc, axis=-1, keepdims=True) + EPS)
    return (y * w.astype(jnp.float32) + b.astype(jnp.float32)).astype(x.dtype)


def ada_modulation(cond, down, up, bias):
    m = (jax.nn.silu(cond) @ down) @ up + bias
    return m.reshape(cond.shape[0], N_MOD, 1, D_MODEL)


def modulate(x, norm_w, m, s):
    return rms_norm(x, norm_w) * (1.0 + m[:, 3 * s + 1]) + m[:, 3 * s]


def swiglu(h, w_in, w_out):
    gu = h @ w_in
    return (jax.nn.silu(gu[..., :D_FF]) * gu[..., D_FF:]) @ w_out


def ffn_sublayer(x, m, s, norm_w, w_in, w_out):
    return x + FFN_RES_WEIGHT * m[:, 3 * s + 2] * swiglu(modulate(x, norm_w, m, s), w_in, w_out)


def fourier_mixer(h, w, b):
    bsz, length, _ = h.shape
    hg = h.astype(jnp.float32).reshape(bsz, length, FOURIER_GROUPS, FOURIER_GROUP_DIM)
    y = jnp.fft.fftn(hg, axes=(1, 3), norm="ortho").real
    y = y.reshape(bsz, length, D_MODEL).astype(h.dtype)
    return y @ w + b


def axial_rope_tables(seq_len):
    n_rows = seq_len // GRID_W
    row = jnp.broadcast_to(jnp.arange(n_rows, dtype=jnp.float32)[:, None], (n_rows, GRID_W)).reshape(-1)
    col = jnp.broadcast_to(jnp.arange(GRID_W, dtype=jnp.float32)[None, :], (n_rows, GRID_W)).reshape(-1)
    freqs = ROPE_THETA ** (-jnp.arange(ROPE_PAIRS, dtype=jnp.float32) / ROPE_PAIRS)
    ang = jnp.stack([row[:, None] * freqs, col[:, None] * freqs], axis=1)
    return jnp.cos(ang)[:, None], jnp.sin(ang)[:, None]


def apply_rope(x, cos, sin):
    bsz, length, heads, _ = x.shape
    xr = x.reshape(bsz, length, heads, 2, 2, ROPE_PAIRS)
    x0, x1 = xr[..., 0, :], xr[..., 1, :]
    cos = cos.astype(x.dtype)
    sin = sin.astype(x.dtype)
    out = jnp.stack([x0 * cos - x1 * sin, x0 * sin + x1 * cos], axis=-2)
    return out.reshape(bsz, length, heads, HEAD_DIM)


def attend(q, k, v):
    s = jnp.einsum('bqhgd,bshd->bhgqs', q, k, preferred_element_type=jnp.float32) * (HEAD_DIM ** -0.5)
    p = jax.nn.softmax(s, axis=-1).astype(v.dtype)
    return jnp.einsum('bhgqs,bshd->bqhgd', p, v)


def attention_mixer(h_x, h_c, w_qkv, q_norm, k_norm, w_o, cos, sin, ctx_out):
    def project(h):
        bsz, length, _ = h.shape
        qkv = h @ w_qkv
        q = qkv[..., :Q_WIDTH].reshape(bsz, length, N_Q_HEADS, HEAD_DIM)
        k = qkv[..., Q_WIDTH:Q_WIDTH + KV_WIDTH].reshape(bsz, length, N_KV_HEADS, HEAD_DIM)
        v = qkv[..., Q_WIDTH + KV_WIDTH:].reshape(bsz, length, N_KV_HEADS, HEAD_DIM)
        return rms_norm(q, q_norm), rms_norm(k, k_norm), v

    q_x, k_x, v_x = project(h_x)
    q_c, k_c, v_c = project(h_c)
    q_x = apply_rope(q_x, cos, sin)
    k_x = apply_rope(k_x, cos, sin)
    k_all = jnp.concatenate([k_x, k_c], axis=1)
    v_all = jnp.concatenate([v_x, v_c], axis=1)
    bsz, length, _ = h_x.shape
    n_blk = length // Q_BLOCK
    qb = q_x.reshape(bsz, n_blk, Q_BLOCK, N_KV_HEADS, GQA_GROUP, HEAD_DIM).transpose(1, 0, 2, 3, 4, 5)
    o = lax.map(lambda q_blk: attend(q_blk, k_all, v_all), qb)
    o = o.transpose(1, 0, 2, 3, 4, 5).reshape(bsz, length, Q_WIDTH)
    y_x = o @ w_o
    if not ctx_out:
        return y_x, None
    lc = h_c.shape[1]
    o_c = attend(q_c.reshape(bsz, lc, N_KV_HEADS, GQA_GROUP, HEAD_DIM), k_c, v_c).reshape(bsz, lc, Q_WIDTH)
    return y_x, o_c @ w_o


def conv_mixer(h, w_pw1, b_pw1, w_dw, b_dw, ln_w, ln_b, w_pw2, b_pw2):
    u = h @ w_pw1 + b_pw1
    u = u[..., :CONV_DIM] * jax.nn.sigmoid(u[..., CONV_DIM:])
    u = lax.conv_general_dilated(
        u, w_dw[:, None, :].astype(u.dtype), window_strides=(1,), padding=[(CONV_PAD, CONV_PAD)],
        dimension_numbers=('NWC', 'WIO', 'NWC'), feature_group_count=CONV_DIM) + b_dw
    u = jax.nn.silu(layer_norm(u, ln_w, ln_b))
    return u @ w_pw2 + b_pw2


def setup_inputs(seed: int = 0) -> dict:
    key = jax.random.key(seed)
    ks = jax.random.split(key, 26)

    def nrm(k, shape, scale):
        return jax.random.normal(k, shape, jnp.float32) * scale

    return {
        "x": nrm(ks[0], (BATCH, SEQ, D_MODEL), 1.0),
        "c": nrm(ks[1], (BATCH, D_MODEL), 1.0),
        "ctx": nrm(ks[2], (BATCH, CTX_LEN, D_MODEL), 1.0),
        "c_ctx": nrm(ks[3], (D_MODEL,), 1.0),
        "ada_down": nrm(ks[4], (DEPTH, D_MODEL, ADA_RANK), D_MODEL ** -0.5),
        "ada_up": nrm(ks[5], (DEPTH, ADA_RANK, N_MOD * D_MODEL), 0.5 * ADA_RANK ** -0.5),
        "ada_b": nrm(ks[6], (DEPTH, N_MOD * D_MODEL), 0.02),
        "norm_w": 1.0 + nrm(ks[7], (DEPTH, N_SUB, D_MODEL), 0.02),
        "ffn_w_in": nrm(ks[8], (DEPTH, 2, D_MODEL, 2 * D_FF), D_MODEL ** -0.5),
        "ffn_w_out": nrm(ks[9], (DEPTH, 2, D_FF, D_MODEL), D_FF ** -0.5),
        "fourier_w": nrm(ks[10], (N_FOURIER_LAYERS, D_MODEL, D_MODEL), D_MODEL ** -0.5),
        "fourier_b": nrm(ks[11], (N_FOURIER_LAYERS, D_MODEL), 0.02),
        "attn_w_qkv": nrm(ks[12], (N_ATTN_LAYERS, D_MODEL, QKV_WIDTH), D_MODEL ** -0.5),
        "attn_q_norm": 1.0 + nrm(ks[13], (N_ATTN_LAYERS, HEAD_DIM), 0.02),
        "attn_k_norm": 1.0 + nrm(ks[14], (N_ATTN_LAYERS, HEAD_DIM), 0.02),
        "attn_w_o": nrm(ks[15], (N_ATTN_LAYERS, Q_WIDTH, D_MODEL), Q_WIDTH ** -0.5),
        "conv_w_pw1": nrm(ks[16], (N_CONV_LAYERS, D_MODEL, 2 * CONV_DIM), D_MODEL ** -0.5),
        "conv_b_pw1": nrm(ks[17], (N_CONV_LAYERS, 2 * CONV_DIM), 0.02),
        "conv_w_dw": nrm(ks[18], (N_CONV_LAYERS, CONV_WIDTH, CONV_DIM), CONV_WIDTH ** -0.5),
        "conv_b_dw": nrm(ks[19], (N_CONV_LAYERS, CONV_DIM), 0.02),
        "conv_ln_w": 1.0 + nrm(ks[20], (N_CONV_LAYERS, CONV_DIM), 0.02),
        "conv_ln_b": nrm(ks[21], (N_CONV_LAYERS, CONV_DIM), 0.02),
        "conv_w_pw2": nrm(ks[22], (N_CONV_LAYERS, CONV_DIM, D_MODEL), CONV_DIM ** -0.5),
        "conv_b_pw2": nrm(ks[23], (N_CONV_LAYERS, D_MODEL), 0.02),
        "final_norm_w": 1.0 + nrm(ks[24], (D_MODEL,), 0.02),
    }


def reference(x, c, ctx, c_ctx, ada_down, ada_up, ada_b, norm_w, ffn_w_in, ffn_w_out,
              fourier_w, fourier_b, attn_w_qkv, attn_q_norm, attn_k_norm, attn_w_o,
              conv_w_pw1, conv_b_pw1, conv_w_dw, conv_b_dw, conv_ln_w, conv_ln_b,
              conv_w_pw2, conv_b_pw2, final_norm_w):
    cos, sin = axial_rope_tables(x.shape[1])
    x_lat, x_ctx = x, ctx
    for i in range(DEPTH):
        kind = i % N_MIXERS
        j = i // N_MIXERS
        last = i == DEPTH - 1
        ctx_in = (not last) or kind == MIXER_ATTN
        ctx_out = not last

        m_x = ada_modulation(c, ada_down[i], ada_up[i], ada_b[i])
        m_c = ada_modulation(c_ctx[None], ada_down[i], ada_up[i], ada_b[i]) if ctx_in else None

        x_lat = ffn_sublayer(x_lat, m_x, 0, norm_w[i, 0], ffn_w_in[i, 0], ffn_w_out[i, 0])
        if ctx_in:
            x_ctx = ffn_sublayer(x_ctx, m_c, 0, norm_w[i, 0], ffn_w_in[i, 0], ffn_w_out[i, 0])

        a_x = modulate(x_lat, norm_w[i, 1], m_x, 1)
        a_c = modulate(x_ctx, norm_w[i, 1], m_c, 1) if ctx_in else None
        if kind == MIXER_FOURIER:
            y_x = fourier_mixer(a_x, fourier_w[j], fourier_b[j])
            y_c = fourier_mixer(a_c, fourier_w[j], fourier_b[j]) if ctx_out else None
        elif kind == MIXER_ATTN:
            y_x, y_c = attention_mixer(a_x, a_c, attn_w_qkv[j], attn_q_norm[j], attn_k_norm[j],
                                       attn_w_o[j], cos, sin, ctx_out)
        else:
            conv_args = (conv_w_pw1[j], conv_b_pw1[j], conv_w_dw[j], conv_b_dw[j],
                         conv_ln_w[j], conv_ln_b[j], conv_w_pw2[j], conv_b_pw2[j])
            y_x = conv_mixer(a_x, *conv_args)
            y_c = conv_mixer(a_c, *conv_args) if ctx_out else None
        x_lat = x_lat + m_x[:, 5] * y_x
        if ctx_out:
            x_ctx = x_ctx + m_c[:, 5] * y_c

        x_lat = ffn_sublayer(x_lat, m_x, 2, norm_w[i, 2], ffn_w_in[i, 1], ffn_w_out[i, 1])
        if ctx_out:
            x_ctx = ffn_sublayer(x_ctx, m_c, 2, norm_w[i, 2], ffn_w_in[i, 1], ffn_w_out[i, 1])

    return rms_norm(x_lat, final_norm_w)
```

```python
import functools
import math

import jax
import jax.numpy as jnp
import numpy as np
from jax import lax
from jax.experimental import pallas as pl
from jax.experimental.pallas import tpu as pltpu

F32 = jnp.float32
BF16 = jnp.bfloat16

HEAD_DIM = 128
GQA_GROUP = 4
GRID_W = 64
FOURIER_GROUPS = 4
N_MIXERS = 3
N_MOD = 9
FFN_RES_WEIGHT = 0.5
ROPE_THETA = 10000.0
EPS = 1e-6

V7X_VMEM_BYTES = 64 * 1024 * 1024
V7X_LANES = 128
F32_SUBLANES = 8
MOD_ROWS = 8
CONV_HALO = 16


def _vmem_limit(*block_bytes, scratch=0):
    need = 2 * sum(block_bytes) + scratch
    return int(min(V7X_VMEM_BYTES - (6 << 20), max(2 * need, 16 << 20)))


def _params(sem, vmem):
    return pltpu.CompilerParams(dimension_semantics=sem, vmem_limit_bytes=vmem)


def _largest_block(n, cap, mult):
    if n <= cap:
        return n
    b = (cap // mult) * mult
    while b >= mult:
        if n % b == 0:
            return b
        b -= mult
    raise ValueError(f"no block for {n} (cap {cap}, multiple {mult})")


def _nbytes(shape, dtype):
    return int(np.prod(shape)) * jnp.dtype(dtype).itemsize


def _silu(x):
    return x * jax.nn.sigmoid(x)


def _ada_down_kernel(c_ref, w_ref, o_ref):
    s = _silu(c_ref[...])
    o_ref[0] = jnp.dot(s.astype(BF16), w_ref[0].astype(BF16), preferred_element_type=F32)


def _ada_up_kernel(h_ref, w_ref, b_ref, o_ref):
    o_ref[0] = jnp.dot(h_ref[0].astype(BF16), w_ref[0].astype(BF16),
                       preferred_element_type=F32) + b_ref[0]


def _ada_modulation(cond, down, up, bias):
    depth, d, rank = down.shape
    nout = up.shape[2]
    bn = _largest_block(rank, 256, V7X_LANES)
    h = pl.pallas_call(
        _ada_down_kernel,
        grid=(depth, rank // bn),
        in_specs=[pl.BlockSpec((MOD_ROWS, d), lambda l, j: (0, 0)),
                  pl.BlockSpec((1, d, bn), lambda l, j: (l, 0, j))],
        out_specs=pl.BlockSpec((1, MOD_ROWS, bn), lambda l, j: (l, 0, j)),
        out_shape=jax.ShapeDtypeStruct((depth, MOD_ROWS, rank), F32),
        compiler_params=_params(("arbitrary", "arbitrary"),
                                _vmem_limit(_nbytes((d, bn), F32), _nbytes((MOD_ROWS, d), F32))),
        name="ada_down",
    )(cond, down)
    bn = _largest_block(nout, 2048, V7X_LANES)
    m = pl.pallas_call(
        _ada_up_kernel,
        grid=(depth, nout // bn),
        in_specs=[pl.BlockSpec((1, MOD_ROWS, rank), lambda l, j: (l, 0, 0)),
                  pl.BlockSpec((1, rank, bn), lambda l, j: (l, 0, j)),
                  pl.BlockSpec((1, 1, bn), lambda l, j: (l, 0, j))],
        out_specs=pl.BlockSpec((1, MOD_ROWS, bn), lambda l, j: (l, 0, j)),
        out_shape=jax.ShapeDtypeStruct((depth, MOD_ROWS, nout), F32),
        compiler_params=_params(("arbitrary", "arbitrary"), _vmem_limit(_nbytes((rank, bn), F32))),
        name="ada_up",
    )(h, up, bias.reshape(depth, 1, nout))
    return m.reshape(depth, MOD_ROWS, N_MOD, d)


class _Groups:
    def __init__(self, first, rows_per_group):
        self.first = first
        self.rows_per_group = rows_per_group

    def index(self, i, bm):
        if self.rows_per_group is None:
            return self.first
        return self.first + (i * bm) // self.rows_per_group

    def block_rows(self, m, cap):
        limit = m if self.rows_per_group is None else self.rows_per_group
        return _largest_block(limit, cap, F32_SUBLANES * 2)


def _norm_mod_kernel(x_ref, nw_ref, sh_ref, sc_ref, o_ref):
    x = x_ref[...]
    y = x * lax.rsqrt(jnp.mean(x * x, axis=-1, keepdims=True) + EPS) * nw_ref[...]
    o_ref[...] = (y * (1.0 + sc_ref[0]) + sh_ref[0]).astype(o_ref.dtype)


def _norm_mod(x, norm_w, shift, scale, groups):
    m, d = x.shape
    bm = groups.block_rows(m, 256)
    r = shift.shape[0]
    mod_spec = pl.BlockSpec((1, 1, d), lambda i: (groups.index(i, bm), 0, 0))
    return pl.pallas_call(
        _norm_mod_kernel,
        grid=(m // bm,),
        in_specs=[pl.BlockSpec((bm, d), lambda i: (i, 0)),
                  pl.BlockSpec((1, d), lambda i: (0, 0)),
                  mod_spec, mod_spec],
        out_specs=pl.BlockSpec((bm, d), lambda i: (i, 0)),
        out_shape=jax.ShapeDtypeStruct((m, d), BF16),
        compiler_params=_params(("arbitrary",), _vmem_limit(_nbytes((bm, d), F32), _nbytes((bm, d), BF16))),
        name="norm_mod",
    )(x, norm_w.reshape(1, d), shift.reshape(r, 1, d), scale.reshape(r, 1, d))


def _rms_norm_kernel(x_ref, nw_ref, o_ref):
    x = x_ref[...]
    o_ref[...] = x * lax.rsqrt(jnp.mean(x * x, axis=-1, keepdims=True) + EPS) * nw_ref[...]


def _rms_norm(x, norm_w):
    m, d = x.shape
    bm = _largest_block(m, 256, F32_SUBLANES)
    return pl.pallas_call(
        _rms_norm_kernel,
        grid=(m // bm,),
        in_specs=[pl.BlockSpec((bm, d), lambda i: (i, 0)), pl.BlockSpec((1, d), lambda i: (0, 0))],
        out_specs=pl.BlockSpec((bm, d), lambda i: (i, 0)),
        out_shape=jax.ShapeDtypeStruct((m, d), F32),
        compiler_params=_params(("arbitrary",), _vmem_limit(2 * _nbytes((bm, d), F32))),
        name="final_rms_norm",
    )(x, norm_w.reshape(1, d))


def _ln_silu_kernel(u_ref, w_ref, b_ref, o_ref):
    u = u_ref[...]
    xc = u - jnp.mean(u, axis=-1, keepdims=True)
    y = xc * lax.rsqrt(jnp.mean(xc * xc, axis=-1, keepdims=True) + EPS)
    o_ref[...] = _silu(y * w_ref[...] + b_ref[...]).astype(o_ref.dtype)


def _ln_silu(u, w, b):
    m, d = u.shape
    bm = _largest_block(m, 256, F32_SUBLANES * 2)
    vec = pl.BlockSpec((1, d), lambda i: (0, 0))
    return pl.pallas_call(
        _ln_silu_kernel,
        grid=(m // bm,),
        in_specs=[pl.BlockSpec((bm, d), lambda i: (i, 0)), vec, vec],
        out_specs=pl.BlockSpec((bm, d), lambda i: (i, 0)),
        out_shape=jax.ShapeDtypeStruct((m, d), BF16),
        compiler_params=_params(("arbitrary",), _vmem_limit(_nbytes((bm, d), F32), _nbytes((bm, d), BF16))),
        name="ln_silu",
    )(u, w.reshape(1, d), b.reshape(1, d))


def _gated_mm_kernel(*refs, mode, has_bias):
    if has_bias:
        a_ref, wa_ref, wb_ref, ba_ref, bb_ref, o_ref = refs
    else:
        a_ref, wa_ref, wb_ref, o_ref = refs
    a = a_ref[...]
    ga = jnp.dot(a, wa_ref[...], preferred_element_type=F32)
    gb = jnp.dot(a, wb_ref[...], preferred_element_type=F32)
    if has_bias:
        ga = ga + ba_ref[...]
        gb = gb + bb_ref[...]
    r = _silu(ga) * gb if mode == "swiglu" else ga * jax.nn.sigmoid(gb)
    o_ref[...] = r.astype(o_ref.dtype)


def _gated_mm(a, w, bias, mode, out_dtype):
    m, k = a.shape
    f = w.shape[1] // 2
    bm = _largest_block(m, 1024, 16)
    bn = _largest_block(f, 512, V7X_LANES)
    nb = f // bn
    in_specs = [pl.BlockSpec((bm, k), lambda i, j: (i, 0)),
                pl.BlockSpec((k, bn), lambda i, j: (0, j)),
                pl.BlockSpec((k, bn), lambda i, j: (0, j + nb))]
    args = [a, w, w]
    if bias is not None:
        b2 = bias.reshape(1, 2 * f)
        in_specs += [pl.BlockSpec((1, bn), lambda i, j: (0, j)),
                     pl.BlockSpec((1, bn), lambda i, j: (0, j + nb))]
        args += [b2, b2]
    return pl.pallas_call(
        functools.partial(_gated_mm_kernel, mode=mode, has_bias=bias is not None),
        grid=(m // bm, nb),
        in_specs=in_specs,
        out_specs=pl.BlockSpec((bm, bn), lambda i, j: (i, j)),
        out_shape=jax.ShapeDtypeStruct((m, f), out_dtype),
        compiler_params=_params(("arbitrary", "arbitrary"),
                                _vmem_limit(_nbytes((bm, k), BF16), 2 * _nbytes((k, bn), BF16),
                                            _nbytes((bm, bn), out_dtype), scratch=3 * _nbytes((bm, bn), F32))),
        name=f"gated_mm_{mode}",
    )(*args)


def _res_mm_kernel(*refs, coef, has_bias):
    if has_bias:
        a_ref, w_ref, b_ref, x_ref, g_ref, o_ref = refs
    else:
        a_ref, w_ref, x_ref, g_ref, o_ref = refs
    y = jnp.dot(a_ref[...], w_ref[...], preferred_element_type=F32)
    if has_bias:
        y = y + b_ref[...]
    g = g_ref[0] if coef == 1.0 else coef * g_ref[0]
    o_ref[...] = x_ref[...] + g * y


def _res_mm(a, w, bias, x, gate, coef, groups):
    m, k = a.shape
    n = w.shape[1]
    bm = groups.block_rows(m, 1024)
    bn = _largest_block(n, 512, V7X_LANES)
    r = gate.shape[0]
    in_specs = [pl.BlockSpec((bm, k), lambda i, j: (i, 0)),
                pl.BlockSpec((k, bn), lambda i, j: (0, j))]
    args = [a, w]
    if bias is not None:
        in_specs.append(pl.BlockSpec((1, bn), lambda i, j: (0, j)))
        args.append(bias.reshape(1, n))
    in_specs += [pl.BlockSpec((bm, bn), lambda i, j: (i, j)),
                 pl.BlockSpec((1, 1, bn), lambda i, j: (groups.index(i, bm), 0, j))]
    args += [x, gate.reshape(r, 1, n)]
    return pl.pallas_call(
        functools.partial(_res_mm_kernel, coef=coef, has_bias=bias is not None),
        grid=(m // bm, n // bn),
        in_specs=in_specs,
        out_specs=pl.BlockSpec((bm, bn), lambda i, j: (i, j)),
        out_shape=jax.ShapeDtypeStruct((m, n), F32),
        compiler_params=_params(("arbitrary", "arbitrary"),
                                _vmem_limit(_nbytes((bm, k), BF16), _nbytes((k, bn), BF16),
                                            2 * _nbytes((bm, bn), F32), scratch=_nbytes((bm, bn), F32))),
        name="res_mm",
    )(*args)


def _rope_tables(seq_len):
    n_rows = seq_len // GRID_W
    pairs = HEAD_DIM // 4
    row = jnp.broadcast_to(jnp.arange(n_rows, dtype=F32)[:, None], (n_rows, GRID_W)).reshape(-1)
    col = jnp.broadcast_to(jnp.arange(GRID_W, dtype=F32)[None, :], (n_rows, GRID_W)).reshape(-1)
    freqs = ROPE_THETA ** (-jnp.arange(pairs, dtype=F32) / pairs)
    ang = jnp.stack([row[:, None] * freqs, col[:, None] * freqs], axis=1)
    cos = jnp.cos(ang)
    sin = jnp.sin(ang)
    cos_t = jnp.stack([cos, cos], axis=2).reshape(seq_len, HEAD_DIM)
    sin_t = jnp.stack([-sin, sin], axis=2).reshape(seq_len, HEAD_DIM)
    return cos_t, sin_t


def _qkv_kernel(*refs, n_norm_blocks, rope):
    if rope:
        a_ref, w_ref, nw_ref, cos_ref, sin_ref, o_ref = refs
    else:
        a_ref, w_ref, nw_ref, o_ref = refs
    j = pl.program_id(1)
    acc = jnp.dot(a_ref[...], w_ref[...], preferred_element_type=F32)
    bn = acc.shape[1]
    pairs = HEAD_DIM // 4

    @pl.when(j < n_norm_blocks)
    def _():
        if rope:
            cos = cos_ref[...]
            sin = sin_ref[...]
            lane = lax.broadcasted_iota(jnp.int32, (1, HEAD_DIM), 1)
            first_half = (lane % (2 * pairs)) < pairs
        for h in range(bn // HEAD_DIM):
            sl = slice(h * HEAD_DIM, (h + 1) * HEAD_DIM)
            t = acc[:, sl]
            y = t * lax.rsqrt(jnp.mean(t * t, axis=-1, keepdims=True) + EPS) * nw_ref[:, sl]
            if rope:
                partner = jnp.where(first_half, pltpu.roll(y, HEAD_DIM - pairs, 1), pltpu.roll(y, pairs, 1))
                y = y * cos + partner * sin
            o_ref[:, sl] = y.astype(o_ref.dtype)

    @pl.when(j >= n_norm_blocks)
    def _():
        o_ref[...] = acc.astype(o_ref.dtype)


def _qkv_project(h, w_qkv, q_norm, k_norm, rope_tables, seq_len):
    m, d = h.shape
    n = w_qkv.shape[1]
    kv_width = (n - d) // 2
    bm = _largest_block(seq_len, 1024, 16)
    bn = _largest_block(math.gcd(d, kv_width), 512, HEAD_DIM)
    nw = jnp.concatenate([jnp.tile(q_norm, d // HEAD_DIM), jnp.tile(k_norm, kv_width // HEAD_DIM),
                          jnp.ones((kv_width,), F32)]).reshape(1, n)
    in_specs = [pl.BlockSpec((bm, d), lambda i, j: (i, 0)),
                pl.BlockSpec((d, bn), lambda i, j: (0, j)),
                pl.BlockSpec((1, bn), lambda i, j: (0, j))]
    args = [h, w_qkv, nw]
    if rope_tables is not None:
        pos_blocks = seq_len // bm
        pos_spec = pl.BlockSpec((bm, HEAD_DIM), lambda i, j: (i % pos_blocks, 0))
        in_specs += [pos_spec, pos_spec]
        args += list(rope_tables)
    return pl.pallas_call(
        functools.partial(_qkv_kernel, n_norm_blocks=(d + kv_width) // bn, rope=rope_tables is not None),
        grid=(m // bm, n // bn),
        in_specs=in_specs,
        out_specs=pl.BlockSpec((bm, bn), lambda i, j: (i, j)),
        out_shape=jax.ShapeDtypeStruct((m, n), BF16),
        compiler_params=_params(("arbitrary", "arbitrary"),
                                _vmem_limit(_nbytes((bm, d), BF16), _nbytes((d, bn), BF16),
                                            _nbytes((bm, bn), BF16), scratch=2 * _nbytes((bm, bn), F32))),
        name="qkv_project",
    )(*args)


def _attn_kernel(q_ref, k_ref, v_ref, o_ref, m_sc, l_sc, acc_sc, *, bk, scale):
    bq = q_ref.shape[1]
    q = jnp.concatenate([q_ref[0, :, g * HEAD_DIM:(g + 1) * HEAD_DIM] for g in range(GQA_GROUP)], axis=0)
    m_sc[...] = jnp.full(m_sc.shape, -jnp.inf, F32)
    l_sc[...] = jnp.zeros(l_sc.shape, F32)
    acc_sc[...] = jnp.zeros(acc_sc.shape, F32)

    def body(kc, carry):
        start = pl.multiple_of(kc * bk, bk)
        kb = k_ref[0, pl.ds(start, bk), :]
        vb = v_ref[0, pl.ds(start, bk), :]
        s = lax.dot_general(q, kb, (((1,), (1,)), ((), ())), preferred_element_type=F32) * scale
        m_prev = m_sc[...]
        m_new = jnp.maximum(m_prev, jnp.max(s, axis=-1, keepdims=True))
        alpha = jnp.exp(m_prev - m_new)
        p = jnp.exp(s - m_new)
        l_sc[...] = alpha * l_sc[...] + jnp.sum(p, axis=-1, keepdims=True)
        acc_sc[...] = alpha * acc_sc[...] + jnp.dot(p.astype(BF16), vb, preferred_element_type=F32)
        m_sc[...] = m_new
        return carry

    lax.fori_loop(0, k_ref.shape[1] // bk, body, 0)
    out = acc_sc[...] / l_sc[...]
    for g in range(GQA_GROUP):
        o_ref[0, :, g * HEAD_DIM:(g + 1) * HEAD_DIM] = out[g * bq:(g + 1) * bq].astype(o_ref.dtype)


def _attention(q, k, v):
    b, lq, qw = q.shape
    lk, kvw = k.shape[1], k.shape[2]
    n_kv = kvw // HEAD_DIM
    gw = GQA_GROUP * HEAD_DIM
    bq = _largest_block(lq, 128, 16)
    bk = _largest_block(lk, 1024, V7X_LANES)
    rows = GQA_GROUP * bq
    kv_spec = pl.BlockSpec((1, lk, HEAD_DIM), lambda bi, hi, qi: (bi, 0, hi))
    return pl.pallas_call(
        functools.partial(_attn_kernel, bk=bk, scale=HEAD_DIM ** -0.5),
        grid=(b, n_kv, lq // bq),
        in_specs=[pl.BlockSpec((1, bq, gw), lambda bi, hi, qi: (bi, qi, hi)), kv_spec, kv_spec],
        out_specs=pl.BlockSpec((1, bq, gw), lambda bi, hi, qi: (bi, qi, hi)),
        out_shape=jax.ShapeDtypeStruct((b, lq, qw), BF16),
        scratch_shapes=[pltpu.VMEM((rows, 1), F32), pltpu.VMEM((rows, 1), F32),
                        pltpu.VMEM((rows, HEAD_DIM), F32)],
        compiler_params=_params(("arbitrary", "arbitrary", "arbitrary"),
                                _vmem_limit(2 * _nbytes((bq, gw), BF16), 2 * _nbytes((lk, HEAD_DIM), BF16),
                                            scratch=4 * _nbytes((rows, bk), F32))),
        name="flash_attention",
    )(q, k, v)


def _dft_tables(n):
    idx = np.arange(n, dtype=np.int64)
    ang = 2.0 * np.pi * ((idx[:, None] * idx[None, :]) % n) / n
    return np.cos(ang), np.sin(ang)


def _split_length(length):
    l2 = 1 << (int(math.log2(length)) // 2)
    assert length % l2 == 0
    return length // l2, l2


def _dft1_kernel(x_ref, c_ref, s_ref, tc_ref, ts_ref, re_ref, im_ref):
    x = x_ref[0]
    u = jnp.dot(c_ref[...], x, preferred_element_type=F32)
    v = jnp.dot(s_ref[...], x, preferred_element_type=F32)
    tc = tc_ref[0]
    ts = ts_ref[0]
    re_ref[0, 0] = (u * tc - v * ts).astype(re_ref.dtype)
    im_ref[0, 0] = (-(v * tc) - u * ts).astype(im_ref.dtype)


def _dft2_kernel(re_ref, im_ref, c_ref, s_ref, p_ref, q_ref):
    tr = re_ref[0]
    ti = im_ref[0]
    c = c_ref[...]
    s = s_ref[...]
    p = jnp.dot(c, tr, preferred_element_type=F32) + jnp.dot(s, ti, preferred_element_type=F32)
    q = jnp.dot(s, tr, preferred_element_type=F32) - jnp.dot(c, ti, preferred_element_type=F32)
    p_ref[0] = p.astype(p_ref.dtype)
    q_ref[0] = q.astype(q_ref.dtype)


def _chan_dft_kernel(p_ref, q_ref, cc_ref, sc_ref, o_ref):
    y = (jnp.dot(p_ref[...], cc_ref[...], preferred_element_type=F32)
         - jnp.dot(q_ref[...], sc_ref[...], preferred_element_type=F32))
    o_ref[...] = y.astype(o_ref.dtype)


def _fourier_real_2d(a, bsz, length):
    m, d = a.shape
    l1, l2 = _split_length(length)
    cg = d // FOURIER_GROUPS
    c2, s2 = (jnp.asarray(t, BF16) for t in _dft_tables(l2))
    c1, s1 = (jnp.asarray(t, BF16) for t in _dft_tables(l1))
    n1 = np.arange(l1, dtype=np.int64)[:, None]
    k2 = np.arange(l2, dtype=np.int64)[None, :]
    tw = 2.0 * np.pi * ((n1 * k2) % length) / length
    tc = jnp.asarray(np.cos(tw), F32).reshape(l1, l2, 1)
    ts = jnp.asarray(np.sin(tw), F32).reshape(l1, l2, 1)
    norm = 1.0 / math.sqrt(length * cg)
    cc, sc = (jnp.asarray(t * norm, BF16) for t in _dft_tables(cg))

    ct = _largest_block(d, 2048, V7X_LANES)
    nct = d // ct
    mat2 = pl.BlockSpec((l2, l2), lambda b, i, j: (0, 0))
    tw_spec = pl.BlockSpec((1, l2, 1), lambda b, i, j: (i, 0, 0))
    t_spec = pl.BlockSpec((1, 1, l2, ct), lambda b, i, j: (b, i, 0, j))
    t_shape = jax.ShapeDtypeStruct((bsz, l1, l2, d), BF16)
    t_re, t_im = pl.pallas_call(
        _dft1_kernel,
        grid=(bsz, l1, nct),
        in_specs=[pl.BlockSpec((1, l2, ct), lambda b, i, j: (b, 0, i * nct + j)), mat2, mat2, tw_spec, tw_spec],
        out_specs=[t_spec, t_spec],
        out_shape=[t_shape, t_shape],
        compiler_params=_params(("arbitrary",) * 3, _vmem_limit(3 * _nbytes((l2, ct), BF16),
                                                                 scratch=4 * _nbytes((l2, ct), F32))),
        name="dft_pos_stage1",
    )(a.reshape(bsz, l2, l1 * d), c2, s2, tc, ts)

    wide = l2 * d
    ct2 = _largest_block(wide, 2048, V7X_LANES)
    mat1 = pl.BlockSpec((l1, l1), lambda b, j: (0, 0))
    blk = pl.BlockSpec((1, l1, ct2), lambda b, j: (b, 0, j))
    pq_shape = jax.ShapeDtypeStruct((bsz, l1, wide), BF16)
    p, q = pl.pallas_call(
        _dft2_kernel,
        grid=(bsz, wide // ct2),
        in_specs=[blk, blk, mat1, mat1],
        out_specs=[blk, blk],
        out_shape=[pq_shape, pq_shape],
        compiler_params=_params(("arbitrary",) * 2, _vmem_limit(4 * _nbytes((l1, ct2), BF16),
                                                                 scratch=4 * _nbytes((l1, ct2), F32))),
        name="dft_pos_stage2",
    )(t_re.reshape(bsz, l1, wide), t_im.reshape(bsz, l1, wide), c1, s1)

    bm = _largest_block(m, 1024, 16)
    pq_spec = pl.BlockSpec((bm, cg), lambda i, g: (i, g))
    mat_c = pl.BlockSpec((cg, cg), lambda i, g: (0, 0))
    return pl.pallas_call(
        _chan_dft_kernel,
        grid=(m // bm, FOURIER_GROUPS),
        in_specs=[pq_spec, pq_spec, mat_c, mat_c],
        out_specs=pq_spec,
        out_shape=jax.ShapeDtypeStruct((m, d), BF16),
        compiler_params=_params(("arbitrary",) * 2, _vmem_limit(3 * _nbytes((bm, cg), BF16),
                                                                 2 * _nbytes((cg, cg), BF16),
                                                                 scratch=2 * _nbytes((bm, cg), F32))),
        name="dft_channel",
    )(p.reshape(m, d), q.reshape(m, d), cc, sc)


def _dwconv_kernel(prev_ref, cur_ref, next_ref, w_ref, b_ref, o_ref, win_sc, *, width, rows_chunk):
    i = pl.program_id(1)
    bt = cur_ref.shape[1]
    pad = (width - 1) // 2
    win_sc[0:CONV_HALO] = jnp.where(i > 0, prev_ref[0], 0.0)
    win_sc[CONV_HALO:CONV_HALO + bt] = cur_ref[0]
    win_sc[CONV_HALO + bt:] = jnp.where(i < pl.num_programs(1) - 1, next_ref[0], 0.0)
    for r in range(bt // rows_chunk):
        base = CONV_HALO - pad + r * rows_chunk
        acc = win_sc[base:base + rows_chunk] * w_ref[0:1]
        for k in range(1, width):
            acc = acc + win_sc[base + k:base + k + rows_chunk] * w_ref[k:k + 1]
        o_ref[0, r * rows_chunk:(r + 1) * rows_chunk] = acc + b_ref[...]


def _depthwise_conv(u, w_dw, b_dw, bsz, length):
    m, c = u.shape
    width = w_dw.shape[0]
    assert (width - 1) // 2 <= CONV_HALO
    bt = _largest_block(length, 128, CONV_HALO)
    cw = _largest_block(c, 512, V7X_LANES)
    rows_chunk = _largest_block(bt, 32, F32_SUBLANES)
    per_block = bt // CONV_HALO
    n_halo = length // CONV_HALO
    u3 = u.reshape(bsz, length, c)
    halo = lambda f: pl.BlockSpec((1, CONV_HALO, cw), f)
    out = pl.pallas_call(
        functools.partial(_dwconv_kernel, width=width, rows_chunk=rows_chunk),
        grid=(bsz, length // bt, c // cw),
        in_specs=[halo(lambda b, i, j: (b, jnp.maximum(i * per_block - 1, 0), j)),
                  pl.BlockSpec((1, bt, cw), lambda b, i, j: (b, i, j)),
                  halo(lambda b, i, j: (b, jnp.minimum((i + 1) * per_block, n_halo - 1), j)),
                  pl.BlockSpec((width, cw), lambda b, i, j: (0, j)),
                  pl.BlockSpec((1, cw), lambda b, i, j: (0, j))],
        out_specs=pl.BlockSpec((1, bt, cw), lambda b, i, j: (b, i, j)),
        out_shape=jax.ShapeDtypeStruct((bsz, length, c), F32),
        scratch_shapes=[pltpu.VMEM((bt + 2 * CONV_HALO, cw), F32)],
        compiler_params=_params(("arbitrary",) * 3, _vmem_limit(4 * _nbytes((bt, cw), F32))),
        name="depthwise_conv",
    )(u3, u3, u3, w_dw, b_dw.reshape(1, c))
    return out.reshape(m, c)


def kernel(x, c, ctx, c_ctx, ada_down, ada_up, ada_b, norm_w, ffn_w_in, ffn_w_out, fourier_w, fourier_b,
           attn_w_qkv, attn_q_norm, attn_k_norm, attn_w_o, conv_w_pw1, conv_b_pw1, conv_w_dw, conv_b_dw,
           conv_ln_w, conv_ln_b, conv_w_pw2, conv_b_pw2, final_norm_w):
    bsz, seq, d = x.shape
    ctx_len = ctx.shape[1]
    depth = ada_down.shape[0]
    assert bsz + 1 <= MOD_ROWS and d % (FOURIER_GROUPS * V7X_LANES) == 0 and seq % GRID_W == 0

    cond = jnp.zeros((MOD_ROWS, d), F32).at[:bsz].set(c).at[bsz].set(c_ctx)
    mod = _ada_modulation(cond, ada_down, ada_up, ada_b)

    lat = _Groups(0, seq)
    con = _Groups(bsz, None)
    x_lat = x.reshape(bsz * seq, d)
    x_ctx = ctx.reshape(bsz * ctx_len, d)
    rope = _rope_tables(seq)
    bf = lambda w: w.astype(BF16)

    def ffn(xs, groups, m, s, nw, w_in, w_out):
        h = _norm_mod(xs, nw, m[:, 3 * s], m[:, 3 * s + 1], groups)
        act = _gated_mm(h, w_in, None, "swiglu", BF16)
        return _res_mm(act, w_out, None, xs, m[:, 3 * s + 2], FFN_RES_WEIGHT, groups)

    for i in range(depth):
        kind = i % N_MIXERS
        j = i // N_MIXERS
        last = i == depth - 1
        ctx_in = (not last) or kind == 1
        ctx_out = not last
        m = mod[i]
        w_in0, w_in1 = bf(ffn_w_in[i, 0]), bf(ffn_w_in[i, 1])
        w_out0, w_out1 = bf(ffn_w_out[i, 0]), bf(ffn_w_out[i, 1])

        x_lat = ffn(x_lat, lat, m, 0, norm_w[i, 0], w_in0, w_out0)
        if ctx_in:
            x_ctx = ffn(x_ctx, con, m, 0, norm_w[i, 0], w_in0, w_out0)

        a_x = _norm_mod(x_lat, norm_w[i, 1], m[:, 3], m[:, 4], lat)
        a_c = _norm_mod(x_ctx, norm_w[i, 1], m[:, 3], m[:, 4], con) if ctx_in else None
        gate = m[:, 5]
        if kind == 0:
            w, b = bf(fourier_w[j]), fourier_b[j]
            x_lat = _res_mm(_fourier_real_2d(a_x, bsz, seq), w, b, x_lat, gate, 1.0, lat)
            if ctx_out:
                x_ctx = _res_mm(_fourier_real_2d(a_c, bsz, ctx_len), w, b, x_ctx, gate, 1.0, con)
        elif kind == 1:
            w_qkv, w_o = bf(attn_w_qkv[j]), bf(attn_w_o[j])
            kvw = (w_qkv.shape[1] - d) // 2
            qkv_x = _qkv_project(a_x, w_qkv, attn_q_norm[j], attn_k_norm[j], rope, seq).reshape(bsz, seq, -1)
            qkv_c = _qkv_project(a_c, w_qkv, attn_q_norm[j], attn_k_norm[j], None, ctx_len)
            qkv_c = qkv_c.reshape(bsz, ctx_len, -1)
            k_all = jnp.concatenate([qkv_x[..., d:d + kvw], qkv_c[..., d:d + kvw]], axis=1)
            v_all = jnp.concatenate([qkv_x[..., d + kvw:], qkv_c[..., d + kvw:]], axis=1)
            o_x = _attention(qkv_x[..., :d], k_all, v_all).reshape(bsz * seq, d)
            x_lat = _res_mm(o_x, w_o, None, x_lat, gate, 1.0, lat)
            if ctx_out:
                o_c = _attention(qkv_c[..., :d], qkv_c[..., d:d + kvw], qkv_c[..., d + kvw:])
                x_ctx = _res_mm(o_c.reshape(bsz * ctx_len, d), w_o, None, x_ctx, gate, 1.0, con)
        else:
            w1, w2 = bf(conv_w_pw1[j]), bf(conv_w_pw2[j])

            def conv(a, length):
                u = _gated_mm(a, w1, conv_b_pw1[j], "glu", F32)
                u = _depthwise_conv(u, conv_w_dw[j], conv_b_dw[j], bsz, length)
                return _ln_silu(u, conv_ln_w[j], conv_ln_b[j])

            x_lat = _res_mm(conv(a_x, seq), w2, conv_b_pw2[j], x_lat, gate, 1.0, lat)
            if ctx_out:
                x_ctx = _res_mm(conv(a_c, ctx_len), w2, conv_b_pw2[j], x_ctx, gate, 1.0, con)

        x_lat = ffn(x_lat, lat, m, 2, norm_w[i, 2], w_in1, w_out1)
        if ctx_out:
            x_ctx = ffn(x_ctx, con, m, 2, norm_w[i, 2], w_in1, w_out1)

    return _rms_norm(x_lat, final_norm_w).reshape(bsz, seq, d)
```

```python
import functools
import math

import jax
import jax.numpy as jnp
import numpy as np
from jax import lax
from jax.experimental import pallas as pl
from jax.experimental.pallas import tpu as pltpu

F32 = jnp.float32
BF16 = jnp.bfloat16

HEAD_DIM = 128
GQA_GROUP = 4
GRID_W = 64
FOURIER_GROUPS = 4
N_MIXERS = 3
N_MOD = 9
FFN_RES_WEIGHT = 0.5
ROPE_THETA = 10000.0
EPS = 1e-6

V7X_VMEM_BYTES = 64 * 1024 * 1024
V7X_LANES = 128
F32_SUBLANES = 8
MOD_ROWS = 8
CONV_HALO = 16


def _vmem_limit(*block_bytes, scratch=0):
    need = 2 * sum(block_bytes) + scratch
    return int(min(V7X_VMEM_BYTES - (6 << 20), max(2 * need, 16 << 20)))


def _params(sem, vmem):
    return pltpu.CompilerParams(dimension_semantics=sem, vmem_limit_bytes=vmem)


def _largest_block(n, cap, mult):
    if n <= cap:
        return n
    b = (cap // mult) * mult
    while b >= mult:
        if n % b == 0:
            return b
        b -= mult
    raise ValueError(f"no block for {n} (cap {cap}, multiple {mult})")


def _nbytes(shape, dtype):
    return int(np.prod(shape)) * jnp.dtype(dtype).itemsize


def _silu(x):
    return x * jax.nn.sigmoid(x)


def _ada_down_kernel(c_ref, w_ref, o_ref):
    s = _silu(c_ref[...])
    o_ref[0] = jnp.dot(s.astype(BF16), w_ref[0].astype(BF16), preferred_element_type=F32)


def _ada_up_kernel(h_ref, w_ref, b_ref, o_ref):
    o_ref[0] = jnp.dot(h_ref[0].astype(BF16), w_ref[0].astype(BF16),
                       preferred_element_type=F32) + b_ref[0]


def _ada_modulation(cond, down, up, bias):
    depth, d, rank = down.shape
    nout = up.shape[2]
    bn = _largest_block(rank, 256, V7X_LANES)
    h = pl.pallas_call(
        _ada_down_kernel,
        grid=(depth, rank // bn),
        in_specs=[pl.BlockSpec((MOD_ROWS, d), lambda l, j: (0, 0)),
                  pl.BlockSpec((1, d, bn), lambda l, j: (l, 0, j))],
        out_specs=pl.BlockSpec((1, MOD_ROWS, bn), lambda l, j: (l, 0, j)),
        out_shape=jax.ShapeDtypeStruct((depth, MOD_ROWS, rank), F32),
        compiler_params=_params(("arbitrary", "arbitrary"),
                                _vmem_limit(_nbytes((d, bn), F32), _nbytes((MOD_ROWS, d), F32))),
        name="ada_down",
    )(cond, down)
    bn = _largest_block(nout, 2048, V7X_LANES)
    m = pl.pallas_call(
        _ada_up_kernel,
        grid=(depth, nout // bn),
        in_specs=[pl.BlockSpec((1, MOD_ROWS, rank), lambda l, j: (l, 0, 0)),
                  pl.BlockSpec((1, rank, bn), lambda l, j: (l, 0, j)),
                  pl.BlockSpec((1, 1, bn), lambda l, j: (l, 0, j))],
        out_specs=pl.BlockSpec((1, MOD_ROWS, bn), lambda l, j: (l, 0, j)),
        out_shape=jax.ShapeDtypeStruct((depth, MOD_ROWS, nout), F32),
        compiler_params=_params(("arbitrary", "arbitrary"), _vmem_limit(_nbytes((rank, bn), F32))),
        name="ada_up",
    )(h, up, bias.reshape(depth, 1, nout))
    return m.reshape(depth, MOD_ROWS, N_MOD, d)


class _Groups:
    def __init__(self, first, rows_per_group):
        self.first = first
        self.rows_per_group = rows_per_group

    def index(self, i, bm):
        if self.rows_per_group is None:
            return self.first
        return self.first + (i * bm) // self.rows_per_group

    def block_rows(self, m, cap):
        limit = m if self.rows_per_group is None else self.rows_per_group
        return _largest_block(limit, cap, F32_SUBLANES * 2)


def _norm_mod_kernel(x_ref, nw_ref, sh_ref, sc_ref, o_ref):
    x = x_ref[...]
    y = x * lax.rsqrt(jnp.mean(x * x, axis=-1, keepdims=True) + EPS) * nw_ref[...]
    o_ref[...] = (y * (1.0 + sc_ref[0]) + sh_ref[0]).astype(o_ref.dtype)


def _norm_mod(x, norm_w, shift, scale, groups):
    m, d = x.shape
    bm = groups.block_rows(m, 256)
    r = shift.shape[0]
    mod_spec = pl.BlockSpec((1, 1, d), lambda i: (groups.index(i, bm), 0, 0))
    return pl.pallas_call(
        _norm_mod_kernel,
        grid=(m // bm,),
        in_specs=[pl.BlockSpec((bm, d), lambda i: (i, 0)),
                  pl.BlockSpec((1, d), lambda i: (0, 0)),
                  mod_spec, mod_spec],
        out_specs=pl.BlockSpec((bm, d), lambda i: (i, 0)),
        out_shape=jax.ShapeDtypeStruct((m, d), BF16),
        compiler_params=_params(("arbitrary",), _vmem_limit(_nbytes((bm, d), F32), _nbytes((bm, d), BF16))),
        name="norm_mod",
    )(x, norm_w.reshape(1, d), shift.reshape(r, 1, d), scale.reshape(r, 1, d))


def _rms_norm_kernel(x_ref, nw_ref, o_ref):
    x = x_ref[...]
    o_ref[...] = x * lax.rsqrt(jnp.mean(x * x, axis=-1, keepdims=True) + EPS) * nw_ref[...]


def _rms_norm(x, norm_w):
    m, d = x.shape
    bm = _largest_block(m, 256, F32_SUBLANES)
    return pl.pallas_call(
        _rms_norm_kernel,
        grid=(m // bm,),
        in_specs=[pl.BlockSpec((bm, d), lambda i: (i, 0)), pl.BlockSpec((1, d), lambda i: (0, 0))],
        out_specs=pl.BlockSpec((bm, d), lambda i: (i, 0)),
        out_shape=jax.ShapeDtypeStruct((m, d), F32),
        compiler_params=_params(("arbitrary",), _vmem_limit(2 * _nbytes((bm, d), F32))),
        name="final_rms_norm",
    )(x, norm_w.reshape(1, d))


def _ln_silu_kernel(u_ref, w_ref, b_ref, o_ref):
    u = u_ref[...]
    xc = u - jnp.mean(u, axis=-1, keepdims=True)
    y = xc * lax.rsqrt(jnp.mean(xc * xc, axis=-1, keepdims=True) + EPS)
    o_ref[...] = _silu(y * w_ref[...] + b_ref[...]).astype(o_ref.dtype)


def _ln_silu(u, w, b):
    m, d = u.shape
    bm = _largest_block(m, 256, F32_SUBLANES * 2)
    vec = pl.BlockSpec((1, d), lambda i: (0, 0))
    return pl.pallas_call(
        _ln_silu_kernel,
        grid=(m // bm,),
        in_specs=[pl.BlockSpec((bm, d), lambda i: (i, 0)), vec, vec],
        out_specs=pl.BlockSpec((bm, d), lambda i: (i, 0)),
        out_shape=jax.ShapeDtypeStruct((m, d), BF16),
        compiler_params=_params(("arbitrary",), _vmem_limit(_nbytes((bm, d), F32), _nbytes((bm, d), BF16))),
        name="ln_silu",
    )(u, w.reshape(1, d), b.reshape(1, d))


def _gated_mm_kernel(*refs, mode, has_bias):
    if has_bias:
        a_ref, wa_ref, wb_ref, ba_ref, bb_ref, o_ref = refs
    else:
        a_ref, wa_ref, wb_ref, o_ref = refs
    a = a_ref[...]
    ga = jnp.dot(a, wa_ref[...], preferred_element_type=F32)
    gb = jnp.dot(a, wb_ref[...], preferred_element_type=F32)
    if has_bias:
        ga = ga + ba_ref[...]
        gb = gb + bb_ref[...]
    r = _silu(ga) * gb if mode == "swiglu" else ga * jax.nn.sigmoid(gb)
    o_ref[...] = r.astype(o_ref.dtype)


def _gated_mm(a, w, bias, mode, out_dtype):
    m, k = a.shape
    f = w.shape[1] // 2
    bm = _largest_block(m, 1024, 16)
    bn = _largest_block(f, 512, V7X_LANES)
    nb = f // bn
    in_specs = [pl.BlockSpec((bm, k), lambda i, j: (i, 0)),
                pl.BlockSpec((k, bn), lambda i, j: (0, j)),
                pl.BlockSpec((k, bn), lambda i, j: (0, j + nb))]
    args = [a, w, w]
    if bias is not None:
        b2 = bias.reshape(1, 2 * f)
        in_specs += [pl.BlockSpec((1, bn), lambda i, j: (0, j)),
                     pl.BlockSpec((1, bn), lambda i, j: (0, j + nb))]
        args += [b2, b2]
    return pl.pallas_call(
        functools.partial(_gated_mm_kernel, mode=mode, has_bias=bias is not None),
        grid=(m // bm, nb),
        in_specs=in_specs,
        out_specs=pl.BlockSpec((bm, bn), lambda i, j: (i, j)),
        out_shape=jax.ShapeDtypeStruct((m, f), out_dtype),
        compiler_params=_params(("arbitrary", "arbitrary"),
                                _vmem_limit(_nbytes((bm, k), BF16), 2 * _nbytes((k, bn), BF16),
                                            _nbytes((bm, bn), out_dtype), scratch=3 * _nbytes((bm, bn), F32))),
        name=f"gated_mm_{mode}",
    )(*args)


def _res_mm_kernel(*refs, coef, has_bias):
    if has_bias:
        a_ref, w_ref, b_ref, x_ref, g_ref, o_ref = refs
    else:
        a_ref, w_ref, x_ref, g_ref, o_ref = refs
    y = jnp.dot(a_ref[...], w_ref[...], preferred_element_type=F32)
    if has_bias:
        y = y + b_ref[...]
    g = g_ref[0] if coef == 1.0 else coef * g_ref[0]
    o_ref[...] = x_ref[...] + g * y


def _res_mm(a, w, bias, x, gate, coef, groups):
    m, k = a.shape
    n = w.shape[1]
    bm = groups.block_rows(m, 1024)
    bn = _largest_block(n, 512, V7X_LANES)
    r = gate.shape[0]
    in_specs = [pl.BlockSpec((bm, k), lambda i, j: (i, 0)),
                pl.BlockSpec((k, bn), lambda i, j: (0, j))]
    args = [a, w]
    if bias is not None:
        in_specs.append(pl.BlockSpec((1, bn), lambda i, j: (0, j)))
        args.append(bias.reshape(1, n))
    in_specs += [pl.BlockSpec((bm, bn), lambda i, j: (i, j)),
                 pl.BlockSpec((1, 1, bn), lambda i, j: (groups.index(i, bm), 0, j))]
    args += [x, gate.reshape(r, 1, n)]
    return pl.pallas_call(
        functools.partial(_res_mm_kernel, coef=coef, has_bias=bias is not None),
        grid=(m // bm, n // bn),
        in_specs=in_specs,
        out_specs=pl.BlockSpec((bm, bn), lambda i, j: (i, j)),
        out_shape=jax.ShapeDtypeStruct((m, n), F32),
        compiler_params=_params(("arbitrary", "arbitrary"),
                                _vmem_limit(_nbytes((bm, k), BF16), _nbytes((k, bn), BF16),
                                            2 * _nbytes((bm, bn), F32), scratch=_nbytes((bm, bn), F32))),
        name="res_mm",
    )(*args)


def _rope_tables(seq_len):
    n_rows = seq_len // GRID_W
    pairs = HEAD_DIM // 4
    row = jnp.broadcast_to(jnp.arange(n_rows, dtype=F32)[:, None], (n_rows, GRID_W)).reshape(-1)
    col = jnp.broadcast_to(jnp.arange(GRID_W, dtype=F32)[None, :], (n_rows, GRID_W)).reshape(-1)
    freqs = ROPE_THETA ** (-jnp.arange(pairs, dtype=F32) / pairs)
    ang = jnp.stack([row[:, None] * freqs, col[:, None] * freqs], axis=1)
    cos = jnp.cos(ang)
    sin = jnp.sin(ang)
    cos_t = jnp.stack([cos, cos], axis=2).reshape(seq_len, HEAD_DIM)
    sin_t = jnp.stack([-sin, sin], axis=2).reshape(seq_len, HEAD_DIM)
    return cos_t, sin_t


def _qkv_kernel(*refs, n_norm_blocks, rope):
    if rope:
        a_ref, w_ref, nw_ref, cos_ref, sin_ref, o_ref = refs
    else:
        a_ref, w_ref, nw_ref, o_ref = refs
    j = pl.program_id(1)
    acc = jnp.dot(a_ref[...], w_ref[...], preferred_element_type=F32)
    bn = acc.shape[1]
    pairs = HEAD_DIM // 4

    @pl.when(j < n_norm_blocks)
    def _():
        if rope:
            cos = cos_ref[...]
            sin = sin_ref[...]
            lane = lax.broadcasted_iota(jnp.int32, (1, HEAD_DIM), 1)
            first_half = (lane % (2 * pairs)) < pairs
        for h in range(bn // HEAD_DIM):
            sl = slice(h * HEAD_DIM, (h + 1) * HEAD_DIM)
            t = acc[:, sl]
            y = t * lax.rsqrt(jnp.mean(t * t, axis=-1, keepdims=True) + EPS) * nw_ref[:, sl]
            if rope:
                partner = jnp.where(first_half, pltpu.roll(y, HEAD_DIM - pairs, 1), pltpu.roll(y, pairs, 1))
                y = y * cos + partner * sin
            o_ref[:, sl] = y.astype(o_ref.dtype)

    @pl.when(j >= n_norm_blocks)
    def _():
        o_ref[...] = acc.astype(o_ref.dtype)


def _qkv_project(h, w_qkv, q_norm, k_norm, rope_tables, seq_len):
    m, d = h.shape
    n = w_qkv.shape[1]
    kv_width = (n - d) // 2
    bm = _largest_block(seq_len, 1024, 16)
    bn = _largest_block(math.gcd(d, kv_width), 512, HEAD_DIM)
    q_scale = HEAD_DIM ** -0.5 * math.log2(math.e)
    nw = jnp.concatenate([jnp.tile(q_norm * q_scale, d // HEAD_DIM), jnp.tile(k_norm, kv_width // HEAD_DIM),
                          jnp.ones((kv_width,), F32)]).reshape(1, n)
    in_specs = [pl.BlockSpec((bm, d), lambda i, j: (i, 0)),
                pl.BlockSpec((d, bn), lambda i, j: (0, j)),
                pl.BlockSpec((1, bn), lambda i, j: (0, j))]
    args = [h, w_qkv, nw]
    if rope_tables is not None:
        pos_blocks = seq_len // bm
        pos_spec = pl.BlockSpec((bm, HEAD_DIM), lambda i, j: (i % pos_blocks, 0))
        in_specs += [pos_spec, pos_spec]
        args += list(rope_tables)
    return pl.pallas_call(
        functools.partial(_qkv_kernel, n_norm_blocks=(d + kv_width) // bn, rope=rope_tables is not None),
        grid=(m // bm, n // bn),
        in_specs=in_specs,
        out_specs=pl.BlockSpec((bm, bn), lambda i, j: (i, j)),
        out_shape=jax.ShapeDtypeStruct((m, n), BF16),
        compiler_params=_params(("arbitrary", "arbitrary"),
                                _vmem_limit(_nbytes((bm, d), BF16), _nbytes((d, bn), BF16),
                                            _nbytes((bm, bn), BF16), scratch=2 * _nbytes((bm, bn), F32))),
        name="qkv_project",
    )(*args)


ATTN_KEY_CHUNK = 1024


def _key_chunks(lengths):
    chunks, col = [], 0
    for src, length in enumerate(lengths):
        size = _largest_block(length, ATTN_KEY_CHUNK, V7X_LANES)
        for start in range(0, length, size):
            chunks.append((src, start, size, col))
            col += size
    return chunks


def _attn_kernel(*refs, n_src):
    q_ref = refs[0]
    k_refs = refs[1:1 + n_src]
    v_refs = refs[1 + n_src:1 + 2 * n_src]
    o_ref, s_sc = refs[1 + 2 * n_src:]
    bq = q_ref.shape[1]
    rows = GQA_GROUP * bq
    chunks = _key_chunks([k.shape[1] for k in k_refs])
    q = jnp.concatenate([q_ref[0, :, g * HEAD_DIM:(g + 1) * HEAD_DIM] for g in range(GQA_GROUP)], axis=0)

    m8 = None
    for src, start, size, col in chunks:
        st = lax.dot_general(k_refs[src][0, start:start + size, :], q, (((1,), (1,)), ((), ())),
                             preferred_element_type=F32)
        s_sc[col:col + size, :] = st
        cm = jnp.max(st.reshape(size // F32_SUBLANES, F32_SUBLANES, rows), axis=0)
        m8 = cm if m8 is None else jnp.maximum(m8, cm)
    m8 = jnp.broadcast_to(jnp.max(m8, axis=0, keepdims=True), (F32_SUBLANES, rows))

    l8 = jnp.zeros((F32_SUBLANES, rows), F32)
    acc = jnp.zeros((HEAD_DIM, rows), F32)
    for src, start, size, col in chunks:
        p = jnp.exp2(s_sc[col:col + size, :].reshape(size // F32_SUBLANES, F32_SUBLANES, rows) - m8[None])
        l8 = l8 + jnp.sum(p, axis=0)
        pt = p.reshape(size, rows).astype(BF16)
        acc = acc + lax.dot_general(v_refs[src][0, start:start + size, :], pt, (((0,), (0,)), ((), ())),
                                    preferred_element_type=F32)
    out_t = acc / jnp.sum(l8, axis=0, keepdims=True)
    for g in range(GQA_GROUP):
        o_ref[0, :, g * HEAD_DIM:(g + 1) * HEAD_DIM] = out_t[:, g * bq:(g + 1) * bq].T.astype(o_ref.dtype)


def _attention(q_src, kv_srcs, d, kv_width):
    b, lq, _ = q_src.shape
    n_kv = kv_width // HEAD_DIM
    gw = GQA_GROUP * HEAD_DIM
    bq = _largest_block(lq, 128, 16)
    rows = GQA_GROUP * bq
    k0 = d // HEAD_DIM
    v0 = (d + kv_width) // HEAD_DIM
    n_keys = sum(a.shape[1] for a in kv_srcs)
    q_spec = pl.BlockSpec((1, bq, gw), lambda bi, hi, qi: (bi, qi, hi))
    k_specs = [pl.BlockSpec((1, a.shape[1], HEAD_DIM), lambda bi, hi, qi: (bi, 0, k0 + hi)) for a in kv_srcs]
    v_specs = [pl.BlockSpec((1, a.shape[1], HEAD_DIM), lambda bi, hi, qi: (bi, 0, v0 + hi)) for a in kv_srcs]
    return pl.pallas_call(
        functools.partial(_attn_kernel, n_src=len(kv_srcs)),
        grid=(b, n_kv, lq // bq),
        in_specs=[q_spec] + k_specs + v_specs,
        out_specs=pl.BlockSpec((1, bq, gw), lambda bi, hi, qi: (bi, qi, hi)),
        out_shape=jax.ShapeDtypeStruct((b, lq, d), BF16),
        scratch_shapes=[pltpu.VMEM((n_keys, rows), F32)],
        compiler_params=_params(("arbitrary", "arbitrary", "arbitrary"),
                                _vmem_limit(2 * _nbytes((bq, gw), BF16), 2 * _nbytes((n_keys, HEAD_DIM), BF16),
                                            scratch=_nbytes((rows, n_keys), F32)
                                            + 2 * _nbytes((rows, ATTN_KEY_CHUNK), F32))),
        name="attention",
    )(q_src, *kv_srcs, *kv_srcs)


DFT_SLABS = 8


def _dft_tables(n):
    idx = np.arange(n, dtype=np.int64)
    ang = 2.0 * np.pi * ((idx[:, None] * idx[None, :]) % n) / n
    return np.cos(ang), np.sin(ang)


def _lincomb(coefs, xs):
    groups = {}
    for cf, x in zip(coefs, xs):
        mag = round(abs(float(cf)), 12)
        if mag == 0.0:
            continue
        pos, neg = groups.setdefault(mag, ([], []))
        (pos if cf > 0 else neg).append(x)
    total = None
    for mag, (pos, neg) in groups.items():
        term = functools.reduce(lambda a, b: a + b, pos) if pos else None
        if neg:
            nsum = functools.reduce(lambda a, b: a + b, neg)
            term = -nsum if term is None else term - nsum
        if mag != 1.0:
            term = term * mag
        total = term if total is None else total + term
    return total


def _dft_slab_kernel(a_ref, tc_ref, ts_ref, re_ref, im_ref):
    cos, sin = _dft_tables(DFT_SLABS)
    for lb in range(a_ref.shape[3] // V7X_LANES):
        sl = slice(lb * V7X_LANES, (lb + 1) * V7X_LANES)
        xs = [a_ref[0, n2, :, sl].astype(F32) for n2 in range(DFT_SLABS)]
        for k2 in range(DFT_SLABS):
            u = _lincomb(cos[k2], xs)
            v = _lincomb(sin[k2], xs)
            tc = tc_ref[k2]
            ts = ts_ref[k2]
            if v is None:
                re, im = u * tc, -(u * ts)
            else:
                re, im = u * tc - v * ts, -(v * tc) - u * ts
            re_ref[0, k2, :, sl] = re.astype(re_ref.dtype)
            im_ref[0, k2, :, sl] = im.astype(im_ref.dtype)


def _dft_dense_kernel(re_ref, im_ref, c_ref, s_ref, nc_ref, p_ref, q_ref):
    tr = re_ref[0, 0]
    ti = im_ref[0, 0]
    s = s_ref[...]
    p = jnp.dot(c_ref[...], tr, preferred_element_type=F32) + jnp.dot(s, ti, preferred_element_type=F32)
    q = jnp.dot(s, tr, preferred_element_type=F32) + jnp.dot(nc_ref[...], ti, preferred_element_type=F32)
    p_ref[0, 0] = p.astype(p_ref.dtype)
    q_ref[0, 0] = q.astype(q_ref.dtype)


def _chan_dft_kernel(p_ref, q_ref, cc_ref, sc_ref, o_ref):
    y = (jnp.dot(p_ref[...], cc_ref[...], preferred_element_type=F32)
         - jnp.dot(q_ref[...], sc_ref[...], preferred_element_type=F32))
    o_ref[...] = y.astype(o_ref.dtype)


def _fourier_real_2d(a, bsz, length):
    m, d = a.shape
    l2 = DFT_SLABS
    l1 = length // l2
    cg = d // FOURIER_GROUPS
    c1, s1 = _dft_tables(l1)
    c1, s1, nc1 = jnp.asarray(c1, BF16), jnp.asarray(s1, BF16), jnp.asarray(-c1, BF16)
    k2 = np.arange(l2, dtype=np.int64)[:, None]
    n1 = np.arange(l1, dtype=np.int64)[None, :]
    tw = 2.0 * np.pi * ((n1 * k2) % length) / length
    lanes = np.ones((1, 1, V7X_LANES))
    tc = jnp.asarray(np.cos(tw)[:, :, None] * lanes, F32)
    ts = jnp.asarray(np.sin(tw)[:, :, None] * lanes, F32)
    norm = 1.0 / math.sqrt(length * cg)
    cc, sc = (jnp.asarray(t * norm, BF16) for t in _dft_tables(cg))

    rt = _largest_block(l1, 32, 16)
    ct = _largest_block(d, 1024, V7X_LANES)
    slab_spec = pl.BlockSpec((1, l2, rt, ct), lambda b, i, j: (b, 0, i, j))
    tw_spec = pl.BlockSpec((l2, rt, V7X_LANES), lambda b, i, j: (0, i, 0))
    slab_shape = jax.ShapeDtypeStruct((bsz, l2, l1, d), BF16)
    t_re, t_im = pl.pallas_call(
        _dft_slab_kernel,
        grid=(bsz, l1 // rt, d // ct),
        in_specs=[slab_spec, tw_spec, tw_spec],
        out_specs=[slab_spec, slab_spec],
        out_shape=[slab_shape, slab_shape],
        compiler_params=_params(("arbitrary",) * 3, _vmem_limit(3 * _nbytes((l2, rt, ct), BF16),
                                                                 2 * _nbytes((l2, rt, V7X_LANES), F32))),
        name="dft_pos_slabs",
    )(a.reshape(bsz, l2, l1, d), tc, ts)

    ct2 = _largest_block(d, 512, V7X_LANES)
    mat1 = pl.BlockSpec((l1, l1), lambda b, s, j: (0, 0))
    blk = pl.BlockSpec((1, 1, l1, ct2), lambda b, s, j: (b, s, 0, j))
    p, q = pl.pallas_call(
        _dft_dense_kernel,
        grid=(bsz, l2, d // ct2),
        in_specs=[blk, blk, mat1, mat1, mat1],
        out_specs=[blk, blk],
        out_shape=[slab_shape, slab_shape],
        compiler_params=_params(("arbitrary",) * 3, _vmem_limit(4 * _nbytes((l1, ct2), BF16),
                                                                 3 * _nbytes((l1, l1), BF16),
                                                                 scratch=4 * _nbytes((l1, ct2), F32))),
        name="dft_pos_dense",
    )(t_re, t_im, c1, s1, nc1)

    bm = _largest_block(m, 1024, 16)
    pq_spec = pl.BlockSpec((bm, cg), lambda i, g: (i, g))
    mat_c = pl.BlockSpec((cg, cg), lambda i, g: (0, 0))
    y = pl.pallas_call(
        _chan_dft_kernel,
        grid=(m // bm, FOURIER_GROUPS),
        in_specs=[pq_spec, pq_spec, mat_c, mat_c],
        out_specs=pq_spec,
        out_shape=jax.ShapeDtypeStruct((m, d), BF16),
        compiler_params=_params(("arbitrary",) * 2, _vmem_limit(3 * _nbytes((bm, cg), BF16),
                                                                 2 * _nbytes((cg, cg), BF16),
                                                                 scratch=2 * _nbytes((bm, cg), F32))),
        name="dft_channel",
    )(p.reshape(m, d), q.reshape(m, d), cc, sc)
    return y.reshape(bsz, l2, l1, d).transpose(0, 2, 1, 3).reshape(m, d)


def _dwconv_kernel(prev_ref, cur_ref, next_ref, w_ref, b_ref, o_ref, win_sc, *, width, rows_chunk):
    i = pl.program_id(1)
    bt = cur_ref.shape[1]
    pad = (width - 1) // 2
    win_sc[0:CONV_HALO] = jnp.where(i > 0, prev_ref[0], 0.0)
    win_sc[CONV_HALO:CONV_HALO + bt] = cur_ref[0]
    win_sc[CONV_HALO + bt:] = jnp.where(i < pl.num_programs(1) - 1, next_ref[0], 0.0)
    for r in range(bt // rows_chunk):
        base = CONV_HALO - pad + r * rows_chunk
        acc = win_sc[base:base + rows_chunk] * w_ref[0:1]
        for k in range(1, width):
            acc = acc + win_sc[base + k:base + k + rows_chunk] * w_ref[k:k + 1]
        o_ref[0, r * rows_chunk:(r + 1) * rows_chunk] = acc + b_ref[...]


def _depthwise_conv(u, w_dw, b_dw, bsz, length):
    m, c = u.shape
    width = w_dw.shape[0]
    assert (width - 1) // 2 <= CONV_HALO
    bt = _largest_block(length, 128, CONV_HALO)
    cw = _largest_block(c, 512, V7X_LANES)
    rows_chunk = _largest_block(bt, 32, F32_SUBLANES)
    per_block = bt // CONV_HALO
    n_halo = length // CONV_HALO
    u3 = u.reshape(bsz, length, c)
    halo = lambda f: pl.BlockSpec((1, CONV_HALO, cw), f)
    out = pl.pallas_call(
        functools.partial(_dwconv_kernel, width=width, rows_chunk=rows_chunk),
        grid=(bsz, length // bt, c // cw),
        in_specs=[halo(lambda b, i, j: (b, jnp.maximum(i * per_block - 1, 0), j)),
                  pl.BlockSpec((1, bt, cw), lambda b, i, j: (b, i, j)),
                  halo(lambda b, i, j: (b, jnp.minimum((i + 1) * per_block, n_halo - 1), j)),
                  pl.BlockSpec((width, cw), lambda b, i, j: (0, j)),
                  pl.BlockSpec((1, cw), lambda b, i, j: (0, j))],
        out_specs=pl.BlockSpec((1, bt, cw), lambda b, i, j: (b, i, j)),
        out_shape=jax.ShapeDtypeStruct((bsz, length, c), F32),
        scratch_shapes=[pltpu.VMEM((bt + 2 * CONV_HALO, cw), F32)],
        compiler_params=_params(("arbitrary",) * 3, _vmem_limit(4 * _nbytes((bt, cw), F32))),
        name="depthwise_conv",
    )(u3, u3, u3, w_dw, b_dw.reshape(1, c))
    return out.reshape(m, c)


def kernel(x, c, ctx, c_ctx, ada_down, ada_up, ada_b, norm_w, ffn_w_in, ffn_w_out, fourier_w, fourier_b,
           attn_w_qkv, attn_q_norm, attn_k_norm, attn_w_o, conv_w_pw1, conv_b_pw1, conv_w_dw, conv_b_dw,
           conv_ln_w, conv_ln_b, conv_w_pw2, conv_b_pw2, final_norm_w):
    bsz, seq, d = x.shape
    ctx_len = ctx.shape[1]
    depth = ada_down.shape[0]
    assert bsz + 1 <= MOD_ROWS and d % (FOURIER_GROUPS * V7X_LANES) == 0 and seq % GRID_W == 0
    assert seq % (16 * DFT_SLABS) == 0 and ctx_len % (16 * DFT_SLABS) == 0

    cond = jnp.zeros((MOD_ROWS, d), F32).at[:bsz].set(c).at[bsz].set(c_ctx)
    mod = _ada_modulation(cond, ada_down, ada_up, ada_b)

    lat = _Groups(0, seq)
    con = _Groups(bsz, None)
    x_lat = x.reshape(bsz * seq, d)
    x_ctx = ctx.reshape(bsz * ctx_len, d)
    rope = _rope_tables(seq)
    bf = lambda w: w.astype(BF16)

    def ffn(xs, groups, m, s, nw, w_in, w_out):
        h = _norm_mod(xs, nw, m[:, 3 * s], m[:, 3 * s + 1], groups)
        act = _gated_mm(h, w_in, None, "swiglu", BF16)
        return _res_mm(act, w_out, None, xs, m[:, 3 * s + 2], FFN_RES_WEIGHT, groups)

    for i in range(depth):
        kind = i % N_MIXERS
        j = i // N_MIXERS
        last = i == depth - 1
        ctx_in = (not last) or kind == 1
        ctx_out = not last
        m = mod[i]
        w_in0, w_in1 = bf(ffn_w_in[i, 0]), bf(ffn_w_in[i, 1])
        w_out0, w_out1 = bf(ffn_w_out[i, 0]), bf(ffn_w_out[i, 1])

        x_lat = ffn(x_lat, lat, m, 0, norm_w[i, 0], w_in0, w_out0)
        if ctx_in:
            x_ctx = ffn(x_ctx, con, m, 0, norm_w[i, 0], w_in0, w_out0)

        a_x = _norm_mod(x_lat, norm_w[i, 1], m[:, 3], m[:, 4], lat)
        a_c = _norm_mod(x_ctx, norm_w[i, 1], m[:, 3], m[:, 4], con) if ctx_in else None
        gate = m[:, 5]
        if kind == 0:
            w, b = bf(fourier_w[j]), fourier_b[j]
            x_lat = _res_mm(_fourier_real_2d(a_x, bsz, seq), w, b, x_lat, gate, 1.0, lat)
            if ctx_out:
                x_ctx = _res_mm(_fourier_real_2d(a_c, bsz, ctx_len), w, b, x_ctx, gate, 1.0, con)
        elif kind == 1:
            w_qkv, w_o = bf(attn_w_qkv[j]), bf(attn_w_o[j])
            kvw = (w_qkv.shape[1] - d) // 2
            qkv_x = _qkv_project(a_x, w_qkv, attn_q_norm[j], attn_k_norm[j], rope, seq).reshape(bsz, seq, -1)
            qkv_c = _qkv_project(a_c, w_qkv, attn_q_norm[j], attn_k_norm[j], None, ctx_len)
            qkv_c = qkv_c.reshape(bsz, ctx_len, -1)
            o_x = _attention(qkv_x, [qkv_x, qkv_c], d, kvw).reshape(bsz * seq, d)
            x_lat = _res_mm(o_x, w_o, None, x_lat, gate, 1.0, lat)
            if ctx_out:
                o_c = _attention(qkv_c, [qkv_c], d, kvw).reshape(bsz * ctx_len, d)
                x_ctx = _res_mm(o_c, w_o, None, x_ctx, gate, 1.0, con)
        else:
            w1, w2 = bf(conv_w_pw1[j]), bf(conv_w_pw2[j])

            def conv(a, length):
                u = _gated_mm(a, w1, conv_b_pw1[j], "glu", F32)
                u = _depthwise_conv(u, conv_w_dw[j], conv_b_dw[j], bsz, length)
                return _ln_silu(u, conv_ln_w[j], conv_ln_b[j])

            x_lat = _res_mm(conv(a_x, seq), w2, conv_b_pw2[j], x_lat, gate, 1.0, lat)
            if ctx_out:
                x_ctx = _res_mm(conv(a_c, ctx_len), w2, conv_b_pw2[j], x_ctx, gate, 1.0, con)

        x_lat = ffn(x_lat, lat, m, 2, norm_w[i, 2], w_in1, w_out1)
        if ctx_out:
            x_ctx = ffn(x_ctx, con, m, 2, norm_w[i, 2], w_in1, w_out1)

    return _rms_norm(x_lat, final_norm_w).reshape(bsz, seq, d)
```

```python
import functools
import math
from typing import NamedTuple

import jax
import jax.numpy as jnp
import numpy as np
from jax import lax
from jax.experimental import pallas as pl
from jax.experimental.pallas import tpu as pltpu

F32 = jnp.float32
BF16 = jnp.bfloat16

HEAD_DIM = 128
GQA_GROUP = 4
GRID_W = 64
FOURIER_GROUPS = 4
N_MIXERS = 3
N_MOD = 9
FFN_RES_WEIGHT = 0.5
ROPE_THETA = 10000.0
EPS = 1e-6

V7X_VMEM_BYTES = 64 * 1024 * 1024
V7X_LANES = 128
F32_SUBLANES = 8
MOD_ROWS = 8
CONV_HALO = 16


def _vmem_limit(*block_bytes, scratch=0):
    need = 2 * sum(block_bytes) + scratch
    return int(min(V7X_VMEM_BYTES - (6 << 20), max(2 * need, 16 << 20)))


def _params(sem, vmem):
    return pltpu.CompilerParams(dimension_semantics=sem, vmem_limit_bytes=vmem)


def _largest_block(n, cap, mult):
    if n <= cap:
        return n
    b = (cap // mult) * mult
    while b >= mult:
        if n % b == 0:
            return b
        b -= mult
    raise ValueError(f"no block for {n} (cap {cap}, multiple {mult})")


def _nbytes(shape, dtype):
    return int(np.prod(shape)) * jnp.dtype(dtype).itemsize


def _silu(x):
    return x * jax.nn.sigmoid(x)


def _ada_down_kernel(c_ref, w_ref, o_ref):
    s = _silu(c_ref[...])
    o_ref[0] = jnp.dot(s.astype(BF16), w_ref[0].astype(BF16), preferred_element_type=F32)


def _ada_up_kernel(h_ref, w_ref, b_ref, o_ref):
    o_ref[0] = jnp.dot(h_ref[0].astype(BF16), w_ref[0].astype(BF16),
                       preferred_element_type=F32) + b_ref[0]


def _ada_modulation(cond, down, up, bias):
    depth, d, rank = down.shape
    nout = up.shape[2]
    bn = _largest_block(rank, 256, V7X_LANES)
    h = pl.pallas_call(
        _ada_down_kernel,
        grid=(depth, rank // bn),
        in_specs=[pl.BlockSpec((MOD_ROWS, d), lambda l, j: (0, 0)),
                  pl.BlockSpec((1, d, bn), lambda l, j: (l, 0, j))],
        out_specs=pl.BlockSpec((1, MOD_ROWS, bn), lambda l, j: (l, 0, j)),
        out_shape=jax.ShapeDtypeStruct((depth, MOD_ROWS, rank), F32),
        compiler_params=_params(("arbitrary", "arbitrary"),
                                _vmem_limit(_nbytes((d, bn), F32), _nbytes((MOD_ROWS, d), F32))),
        name="ada_down",
    )(cond, down)
    bn = _largest_block(nout, 2048, V7X_LANES)
    m = pl.pallas_call(
        _ada_up_kernel,
        grid=(depth, nout // bn),
        in_specs=[pl.BlockSpec((1, MOD_ROWS, rank), lambda l, j: (l, 0, 0)),
                  pl.BlockSpec((1, rank, bn), lambda l, j: (l, 0, j)),
                  pl.BlockSpec((1, 1, bn), lambda l, j: (l, 0, j))],
        out_specs=pl.BlockSpec((1, MOD_ROWS, bn), lambda l, j: (l, 0, j)),
        out_shape=jax.ShapeDtypeStruct((depth, MOD_ROWS, nout), F32),
        compiler_params=_params(("arbitrary", "arbitrary"), _vmem_limit(_nbytes((rank, bn), F32))),
        name="ada_up",
    )(h, up, bias.reshape(depth, 1, nout))
    return m.reshape(depth, MOD_ROWS, N_MOD, d)


class _Groups:
    def __init__(self, first, rows_per_group):
        self.first = first
        self.rows_per_group = rows_per_group

    def index(self, i, bm):
        if self.rows_per_group is None:
            return self.first
        return self.first + (i * bm) // self.rows_per_group

    def block_rows(self, m, cap):
        limit = m if self.rows_per_group is None else self.rows_per_group
        return _largest_block(limit, cap, F32_SUBLANES * 2)


class _Stream(NamedTuple):
    x: jax.Array
    xb: jax.Array
    sq: jax.Array


class _Norm(NamedTuple):
    w: jax.Array
    shift: jax.Array
    scale: jax.Array
    groups: _Groups

    def specs_and_args(self, bm, d):
        r = self.shift.shape[0]
        g = self.groups
        mod = pl.BlockSpec((1, 1, d), lambda i, *_: (g.index(i, bm), 0, 0))
        return ([pl.BlockSpec((1, d), lambda i, *_: (0, 0)), mod, mod],
                [self.w.reshape(1, d), self.shift.reshape(r, 1, d), self.scale.reshape(r, 1, d)])


NORM_ROWS = 32


def _lane_partial_sq(x):
    sq = x * x
    part = sq[:, :V7X_LANES]
    for t in range(1, x.shape[1] // V7X_LANES):
        part = part + sq[:, t * V7X_LANES:(t + 1) * V7X_LANES]
    return part


def _normalize_into(h_ref, xb_ref, sq_ref, nw_ref, sh_ref, sc_ref):
    d = xb_ref.shape[1]

    def chunk(r, carry):
        rows = pl.ds(pl.multiple_of(r * NORM_ROWS, NORM_ROWS), NORM_ROWS)
        rstd = lax.rsqrt(jnp.sum(sq_ref[rows, :], axis=-1, keepdims=True) / d + EPS)
        y = xb_ref[rows, :].astype(F32) * rstd * nw_ref[...]
        h_ref[rows, :] = (y * (1.0 + sc_ref[0]) + sh_ref[0]).astype(h_ref.dtype)
        return carry

    lax.fori_loop(0, xb_ref.shape[0] // NORM_ROWS, chunk, 0)


def _prep_kernel(x_ref, xb_ref, sq_ref):
    x = x_ref[...]
    xb_ref[...] = x.astype(xb_ref.dtype)
    sq_ref[...] = _lane_partial_sq(x)


def _prep_stream(x):
    m, d = x.shape
    bm = _largest_block(m, 256, NORM_ROWS)
    row = lambda n: pl.BlockSpec((bm, n), lambda i: (i, 0))
    xb, sq = pl.pallas_call(
        _prep_kernel,
        grid=(m // bm,),
        in_specs=[row(d)],
        out_specs=[row(d), row(V7X_LANES)],
        out_shape=[jax.ShapeDtypeStruct((m, d), BF16), jax.ShapeDtypeStruct((m, V7X_LANES), F32)],
        compiler_params=_params(("arbitrary",), _vmem_limit(_nbytes((bm, d), F32), _nbytes((bm, d), BF16))),
        name="prep_stream",
    )(x)
    return _Stream(x, xb, sq)


def _norm_apply_kernel(xb_ref, sq_ref, nw_ref, sh_ref, sc_ref, o_ref):
    _normalize_into(o_ref, xb_ref, sq_ref, nw_ref, sh_ref, sc_ref)


def _norm_apply(stream, norm):
    m, d = stream.xb.shape
    bm = norm.groups.block_rows(m, 256)
    row = lambda n: pl.BlockSpec((bm, n), lambda i: (i, 0))
    nspecs, nargs = norm.specs_and_args(bm, d)
    return pl.pallas_call(
        _norm_apply_kernel,
        grid=(m // bm,),
        in_specs=[row(d), row(V7X_LANES)] + nspecs,
        out_specs=row(d),
        out_shape=jax.ShapeDtypeStruct((m, d), BF16),
        compiler_params=_params(("arbitrary",), _vmem_limit(2 * _nbytes((bm, d), BF16),
                                                            scratch=4 * _nbytes((NORM_ROWS, d), F32))),
        name="norm_apply",
    )(stream.xb, stream.sq, *nargs)


def _rms_norm_kernel(x_ref, nw_ref, o_ref):
    x = x_ref[...]
    o_ref[...] = x * lax.rsqrt(jnp.mean(x * x, axis=-1, keepdims=True) + EPS) * nw_ref[...]


def _rms_norm(x, norm_w):
    m, d = x.shape
    bm = _largest_block(m, 256, F32_SUBLANES)
    return pl.pallas_call(
        _rms_norm_kernel,
        grid=(m // bm,),
        in_specs=[pl.BlockSpec((bm, d), lambda i: (i, 0)), pl.BlockSpec((1, d), lambda i: (0, 0))],
        out_specs=pl.BlockSpec((bm, d), lambda i: (i, 0)),
        out_shape=jax.ShapeDtypeStruct((m, d), F32),
        compiler_params=_params(("arbitrary",), _vmem_limit(2 * _nbytes((bm, d), F32))),
        name="final_rms_norm",
    )(x, norm_w.reshape(1, d))


def _ln_silu_kernel(u_ref, w_ref, b_ref, o_ref):
    u = u_ref[...]
    xc = u - jnp.mean(u, axis=-1, keepdims=True)
    y = xc * lax.rsqrt(jnp.mean(xc * xc, axis=-1, keepdims=True) + EPS)
    o_ref[...] = _silu(y * w_ref[...] + b_ref[...]).astype(o_ref.dtype)


def _ln_silu(u, w, b):
    m, d = u.shape
    bm = _largest_block(m, 256, F32_SUBLANES * 2)
    vec = pl.BlockSpec((1, d), lambda i: (0, 0))
    return pl.pallas_call(
        _ln_silu_kernel,
        grid=(m // bm,),
        in_specs=[pl.BlockSpec((bm, d), lambda i: (i, 0)), vec, vec],
        out_specs=pl.BlockSpec((bm, d), lambda i: (i, 0)),
        out_shape=jax.ShapeDtypeStruct((m, d), BF16),
        compiler_params=_params(("arbitrary",), _vmem_limit(_nbytes((bm, d), F32), _nbytes((bm, d), BF16))),
        name="ln_silu",
    )(u, w.reshape(1, d), b.reshape(1, d))


class _Weight(NamedTuple):
    array: jax.Array
    lead: tuple

    @property
    def shape(self):
        return self.array.shape[len(self.lead):]

    def spec(self, bn, col):
        lead = self.lead
        k = self.shape[0]
        return pl.BlockSpec((None,) * len(lead) + (k, bn), lambda i, j: (*lead, 0, col(i, j)))


def _gated_mm_kernel(*refs, mode, has_bias):
    xb_ref, sq_ref, nw_ref, sh_ref, sc_ref, wa_ref, wb_ref = refs[:7]
    o_ref, h_sc = refs[-2:]

    @pl.when(pl.program_id(1) == 0)
    def _():
        _normalize_into(h_sc, xb_ref, sq_ref, nw_ref, sh_ref, sc_ref)

    a = h_sc[...]
    ga = jnp.dot(a, wa_ref[...], preferred_element_type=F32)
    gb = jnp.dot(a, wb_ref[...], preferred_element_type=F32)
    if has_bias:
        ga = ga + refs[7][...]
        gb = gb + refs[8][...]
    r = _silu(ga) * gb if mode == "swiglu" else ga * jax.nn.sigmoid(gb)
    o_ref[...] = r.astype(o_ref.dtype)


def _gated_mm(stream, norm, w, bias, mode, out_dtype):
    m, k = stream.xb.shape
    f = w.shape[1] // 2
    bm = norm.groups.block_rows(m, 1024)
    bn = _largest_block(f, 512, V7X_LANES)
    nb = f // bn
    nspecs, nargs = norm.specs_and_args(bm, k)
    in_specs = [pl.BlockSpec((bm, k), lambda i, j: (i, 0)),
                pl.BlockSpec((bm, V7X_LANES), lambda i, j: (i, 0))] + nspecs
    in_specs += [w.spec(bn, lambda i, j: j), w.spec(bn, lambda i, j: j + nb)]
    args = [stream.xb, stream.sq] + nargs + [w.array, w.array]
    if bias is not None:
        b2 = bias.reshape(1, 2 * f)
        in_specs += [pl.BlockSpec((1, bn), lambda i, j: (0, j)),
                     pl.BlockSpec((1, bn), lambda i, j: (0, j + nb))]
        args += [b2, b2]
    return pl.pallas_call(
        functools.partial(_gated_mm_kernel, mode=mode, has_bias=bias is not None),
        grid=(m // bm, nb),
        in_specs=in_specs,
        out_specs=pl.BlockSpec((bm, bn), lambda i, j: (i, j)),
        out_shape=jax.ShapeDtypeStruct((m, f), out_dtype),
        scratch_shapes=[pltpu.VMEM((bm, k), BF16)],
        compiler_params=_params(("arbitrary", "arbitrary"),
                                _vmem_limit(_nbytes((bm, k), BF16), 2 * _nbytes((k, bn), BF16),
                                            _nbytes((bm, bn), out_dtype),
                                            scratch=_nbytes((bm, k), BF16) + 3 * _nbytes((bm, bn), F32))),
        name=f"gated_mm_{mode}",
    )(*args)


def _res_mm_kernel(*refs, coef, has_bias, emit_stream):
    a_ref, w_ref = refs[:2]
    n_in = 5 if has_bias else 4
    x_ref, g_ref = refs[n_in - 2:n_in]
    o_ref = refs[n_in]
    y = jnp.dot(a_ref[...], w_ref[...], preferred_element_type=F32)
    if has_bias:
        y = y + refs[2][...]
    g = g_ref[0] if coef == 1.0 else coef * g_ref[0]
    xn = x_ref[...] + g * y
    o_ref[...] = xn
    if emit_stream:
        xb_ref, sq_ref = refs[n_in + 1:]
        xb_ref[...] = xn.astype(xb_ref.dtype)
        part = _lane_partial_sq(xn)
        j = pl.program_id(1)

        @pl.when(j == 0)
        def _():
            sq_ref[...] = part

        @pl.when(j > 0)
        def _():
            sq_ref[...] += part


def _res_mm(a, w, bias, stream, gate, coef, groups, emit_stream=True):
    x = stream.x
    m, k = a.shape
    n = w.shape[1]
    bm = groups.block_rows(m, 1024)
    bn = _largest_block(n, 512, V7X_LANES)
    r = gate.shape[0]
    in_specs = [pl.BlockSpec((bm, k), lambda i, j: (i, 0)), w.spec(bn, lambda i, j: j)]
    args = [a, w.array]
    if bias is not None:
        in_specs.append(pl.BlockSpec((1, bn), lambda i, j: (0, j)))
        args.append(bias.reshape(1, n))
    tile = pl.BlockSpec((bm, bn), lambda i, j: (i, j))
    in_specs += [tile, pl.BlockSpec((1, 1, bn), lambda i, j: (groups.index(i, bm), 0, j))]
    args += [x, gate.reshape(r, 1, n)]
    out_specs = [tile]
    out_shape = [jax.ShapeDtypeStruct((m, n), F32)]
    if emit_stream:
        out_specs += [tile, pl.BlockSpec((bm, V7X_LANES), lambda i, j: (i, 0))]
        out_shape += [jax.ShapeDtypeStruct((m, n), BF16), jax.ShapeDtypeStruct((m, V7X_LANES), F32)]
    out = pl.pallas_call(
        functools.partial(_res_mm_kernel, coef=coef, has_bias=bias is not None, emit_stream=emit_stream),
        grid=(m // bm, n // bn),
        in_specs=in_specs,
        out_specs=out_specs,
        out_shape=out_shape,
        compiler_params=_params(("arbitrary", "arbitrary"),
                                _vmem_limit(_nbytes((bm, k), BF16), _nbytes((k, bn), BF16),
                                            3 * _nbytes((bm, bn), F32), scratch=_nbytes((bm, bn), F32))),
        name="res_mm",
    )(*args)
    return _Stream(*out) if emit_stream else _Stream(out[0], None, None)


def _rope_tables(seq_len):
    n_rows = seq_len // GRID_W
    pairs = HEAD_DIM // 4
    row = jnp.broadcast_to(jnp.arange(n_rows, dtype=F32)[:, None], (n_rows, GRID_W)).reshape(-1)
    col = jnp.broadcast_to(jnp.arange(GRID_W, dtype=F32)[None, :], (n_rows, GRID_W)).reshape(-1)
    freqs = ROPE_THETA ** (-jnp.arange(pairs, dtype=F32) / pairs)
    ang = jnp.stack([row[:, None] * freqs, col[:, None] * freqs], axis=1)
    cos = jnp.cos(ang)
    sin = jnp.sin(ang)
    cos_t = jnp.stack([cos, cos], axis=2).reshape(seq_len, HEAD_DIM)
    sin_t = jnp.stack([-sin, sin], axis=2).reshape(seq_len, HEAD_DIM)
    return cos_t, sin_t


def _qkv_kernel(*refs, n_norm_blocks, rope):
    xb_ref, sq_ref, mw_ref, sh_ref, sc_ref, w_ref, nw_ref = refs[:7]
    if rope:
        cos_ref, sin_ref = refs[7:9]
    o_ref, h_sc = refs[-2:]
    j = pl.program_id(1)

    @pl.when(j == 0)
    def _():
        _normalize_into(h_sc, xb_ref, sq_ref, mw_ref, sh_ref, sc_ref)

    acc = jnp.dot(h_sc[...], w_ref[...], preferred_element_type=F32)
    bn = acc.shape[1]
    pairs = HEAD_DIM // 4

    @pl.when(j < n_norm_blocks)
    def _():
        if rope:
            cos = cos_ref[...]
            sin = sin_ref[...]
            lane = lax.broadcasted_iota(jnp.int32, (1, HEAD_DIM), 1)
            first_half = (lane % (2 * pairs)) < pairs
        for h in range(bn // HEAD_DIM):
            sl = slice(h * HEAD_DIM, (h + 1) * HEAD_DIM)
            t = acc[:, sl]
            y = t * lax.rsqrt(jnp.mean(t * t, axis=-1, keepdims=True) + EPS) * nw_ref[:, sl]
            if rope:
                partner = jnp.where(first_half, pltpu.roll(y, HEAD_DIM - pairs, 1), pltpu.roll(y, pairs, 1))
                y = y * cos + partner * sin
            o_ref[:, sl] = y.astype(o_ref.dtype)

    @pl.when(j >= n_norm_blocks)
    def _():
        o_ref[...] = acc.astype(o_ref.dtype)


def _qkv_project(stream, norm, w_qkv, q_norm, k_norm, rope_tables, seq_len):
    m, d = stream.xb.shape
    n = w_qkv.shape[1]
    kv_width = (n - d) // 2
    bm = _largest_block(seq_len, 1024, NORM_ROWS)
    bn = _largest_block(math.gcd(d, kv_width), 512, HEAD_DIM)
    q_scale = HEAD_DIM ** -0.5 * math.log2(math.e)
    nw = jnp.concatenate([jnp.tile(q_norm * q_scale, d // HEAD_DIM), jnp.tile(k_norm, kv_width // HEAD_DIM),
                          jnp.ones((kv_width,), F32)]).reshape(1, n)
    nspecs, nargs = norm.specs_and_args(bm, d)
    in_specs = [pl.BlockSpec((bm, d), lambda i, j: (i, 0)),
                pl.BlockSpec((bm, V7X_LANES), lambda i, j: (i, 0))] + nspecs
    in_specs += [w_qkv.spec(bn, lambda i, j: j), pl.BlockSpec((1, bn), lambda i, j: (0, j))]
    args = [stream.xb, stream.sq] + nargs + [w_qkv.array, nw]
    if rope_tables is not None:
        pos_blocks = seq_len // bm
        pos_spec = pl.BlockSpec((bm, HEAD_DIM), lambda i, j: (i % pos_blocks, 0))
        in_specs += [pos_spec, pos_spec]
        args += list(rope_tables)
    return pl.pallas_call(
        functools.partial(_qkv_kernel, n_norm_blocks=(d + kv_width) // bn, rope=rope_tables is not None),
        grid=(m // bm, n // bn),
        in_specs=in_specs,
        out_specs=pl.BlockSpec((bm, bn), lambda i, j: (i, j)),
        out_shape=jax.ShapeDtypeStruct((m, n), BF16),
        scratch_shapes=[pltpu.VMEM((bm, d), BF16)],
        compiler_params=_params(("arbitrary", "arbitrary"),
                                _vmem_limit(_nbytes((bm, d), BF16), _nbytes((d, bn), BF16), _nbytes((bm, bn), BF16),
                                            scratch=_nbytes((bm, d), BF16) + 2 * _nbytes((bm, bn), F32))),
        name="qkv_project",
    )(*args)


ATTN_KEY_CHUNK = 1024


def _key_chunks(lengths):
    chunks, col = [], 0
    for src, length in enumerate(lengths):
        size = _largest_block(length, ATTN_KEY_CHUNK, V7X_LANES)
        for start in range(0, length, size):
            chunks.append((src, start, size, col))
            col += size
    return chunks


def _attn_kernel(*refs, n_src):
    q_ref = refs[0]
    k_refs = refs[1:1 + n_src]
    v_refs = refs[1 + n_src:1 + 2 * n_src]
    o_ref, s_sc = refs[1 + 2 * n_src:]
    bq = q_ref.shape[1]
    rows = GQA_GROUP * bq
    chunks = _key_chunks([k.shape[1] for k in k_refs])
    q = jnp.concatenate([q_ref[0, :, g * HEAD_DIM:(g + 1) * HEAD_DIM] for g in range(GQA_GROUP)], axis=0)

    m8 = None
    for src, start, size, col in chunks:
        st = lax.dot_general(k_refs[src][0, start:start + size, :], q, (((1,), (1,)), ((), ())),
                             preferred_element_type=F32)
        s_sc[col:col + size, :] = st
        cm = jnp.max(st.reshape(size // F32_SUBLANES, F32_SUBLANES, rows), axis=0)
        m8 = cm if m8 is None else jnp.maximum(m8, cm)
    m8 = jnp.broadcast_to(jnp.max(m8, axis=0, keepdims=True), (F32_SUBLANES, rows))

    l8 = jnp.zeros((F32_SUBLANES, rows), F32)
    acc = jnp.zeros((HEAD_DIM, rows), F32)
    for src, start, size, col in chunks:
        p = jnp.exp2(s_sc[col:col + size, :].reshape(size // F32_SUBLANES, F32_SUBLANES, rows) - m8[None])
        l8 = l8 + jnp.sum(p, axis=0)
        pt = p.reshape(size, rows).astype(BF16)
        acc = acc + lax.dot_general(v_refs[src][0, start:start + size, :], pt, (((0,), (0,)), ((), ())),
                                    preferred_element_type=F32)
    out_t = acc / jnp.sum(l8, axis=0, keepdims=True)
    for g in range(GQA_GROUP):
        o_ref[0, :, g * HEAD_DIM:(g + 1) * HEAD_DIM] = out_t[:, g * bq:(g + 1) * bq].T.astype(o_ref.dtype)


def _attention(q_src, kv_srcs, d, kv_width):
    b, lq, _ = q_src.shape
    n_kv = kv_width // HEAD_DIM
    gw = GQA_GROUP * HEAD_DIM
    bq = _largest_block(lq, 128, 16)
    rows = GQA_GROUP * bq
    k0 = d // HEAD_DIM
    v0 = (d + kv_width) // HEAD_DIM
    n_keys = sum(a.shape[1] for a in kv_srcs)
    q_spec = pl.BlockSpec((1, bq, gw), lambda bi, hi, qi: (bi, qi, hi))
    k_specs = [pl.BlockSpec((1, a.shape[1], HEAD_DIM), lambda bi, hi, qi: (bi, 0, k0 + hi)) for a in kv_srcs]
    v_specs = [pl.BlockSpec((1, a.shape[1], HEAD_DIM), lambda bi, hi, qi: (bi, 0, v0 + hi)) for a in kv_srcs]
    return pl.pallas_call(
        functools.partial(_attn_kernel, n_src=len(kv_srcs)),
        grid=(b, n_kv, lq // bq),
        in_specs=[q_spec] + k_specs + v_specs,
        out_specs=pl.BlockSpec((1, bq, gw), lambda bi, hi, qi: (bi, qi, hi)),
        out_shape=jax.ShapeDtypeStruct((b, lq, d), BF16),
        scratch_shapes=[pltpu.VMEM((n_keys, rows), F32)],
        compiler_params=_params(("arbitrary", "arbitrary", "arbitrary"),
                                _vmem_limit(2 * _nbytes((bq, gw), BF16), 2 * _nbytes((n_keys, HEAD_DIM), BF16),
                                            scratch=_nbytes((rows, n_keys), F32)
                                            + 2 * _nbytes((rows, ATTN_KEY_CHUNK), F32))),
        name="attention",
    )(q_src, *kv_srcs, *kv_srcs)


DFT_SLABS = 8


def _dft_tables(n):
    idx = np.arange(n, dtype=np.int64)
    ang = 2.0 * np.pi * ((idx[:, None] * idx[None, :]) % n) / n
    return np.cos(ang), np.sin(ang)


def _lincomb(coefs, xs):
    groups = {}
    for cf, x in zip(coefs, xs):
        mag = round(abs(float(cf)), 12)
        if mag == 0.0:
            continue
        pos, neg = groups.setdefault(mag, ([], []))
        (pos if cf > 0 else neg).append(x)
    total = None
    for mag, (pos, neg) in groups.items():
        term = functools.reduce(lambda a, b: a + b, pos) if pos else None
        if neg:
            nsum = functools.reduce(lambda a, b: a + b, neg)
            term = -nsum if term is None else term - nsum
        if mag != 1.0:
            term = term * mag
        total = term if total is None else total + term
    return total


def _dft_slab_kernel(a_ref, tc_ref, ts_ref, re_ref, im_ref):
    cos, sin = _dft_tables(DFT_SLABS)
    for lb in range(a_ref.shape[3] // V7X_LANES):
        sl = slice(lb * V7X_LANES, (lb + 1) * V7X_LANES)
        xs = [a_ref[0, n2, :, sl].astype(F32) for n2 in range(DFT_SLABS)]
        for k2 in range(DFT_SLABS):
            u = _lincomb(cos[k2], xs)
            v = _lincomb(sin[k2], xs)
            tc = tc_ref[k2]
            ts = ts_ref[k2]
            if v is None:
                re, im = u * tc, -(u * ts)
            else:
                re, im = u * tc - v * ts, -(v * tc) - u * ts
            re_ref[0, k2, :, sl] = re.astype(re_ref.dtype)
            im_ref[0, k2, :, sl] = im.astype(im_ref.dtype)


def _dft_dense_kernel(re_ref, im_ref, cs_ref, snc_ref, p_ref, q_ref):
    t = jnp.concatenate([re_ref[0, 0], im_ref[0, 0]], axis=0)
    p_ref[0, 0] = jnp.dot(cs_ref[...], t, preferred_element_type=F32).astype(p_ref.dtype)
    q_ref[0, 0] = jnp.dot(snc_ref[...], t, preferred_element_type=F32).astype(q_ref.dtype)


def _chan_dft_kernel(p_ref, q_ref, csn_ref, o_ref):
    pq = jnp.concatenate([p_ref[...], q_ref[...]], axis=1)
    o_ref[...] = jnp.dot(pq, csn_ref[...], preferred_element_type=F32).astype(o_ref.dtype)


def _fourier_real_2d(a, bsz, length):
    m, d = a.shape
    l2 = DFT_SLABS
    l1 = length // l2
    cg = d // FOURIER_GROUPS
    c1, s1 = _dft_tables(l1)
    cs1 = jnp.asarray(np.concatenate([c1, s1], axis=1), BF16)
    snc1 = jnp.asarray(np.concatenate([s1, -c1], axis=1), BF16)
    k2 = np.arange(l2, dtype=np.int64)[:, None]
    n1 = np.arange(l1, dtype=np.int64)[None, :]
    tw = 2.0 * np.pi * ((n1 * k2) % length) / length
    lanes = np.ones((1, 1, V7X_LANES))
    tc = jnp.asarray(np.cos(tw)[:, :, None] * lanes, F32)
    ts = jnp.asarray(np.sin(tw)[:, :, None] * lanes, F32)
    norm = 1.0 / math.sqrt(length * cg)
    cc, sc = _dft_tables(cg)
    csn = jnp.asarray(np.concatenate([cc, -sc], axis=0) * norm, BF16)

    rt = _largest_block(l1, 32, 16)
    ct = _largest_block(d, 1024, V7X_LANES)
    slab_spec = pl.BlockSpec((1, l2, rt, ct), lambda b, i, j: (b, 0, i, j))
    tw_spec = pl.BlockSpec((l2, rt, V7X_LANES), lambda b, i, j: (0, i, 0))
    slab_shape = jax.ShapeDtypeStruct((bsz, l2, l1, d), BF16)
    t_re, t_im = pl.pallas_call(
        _dft_slab_kernel,
        grid=(bsz, l1 // rt, d // ct),
        in_specs=[slab_spec, tw_spec, tw_spec],
        out_specs=[slab_spec, slab_spec],
        out_shape=[slab_shape, slab_shape],
        compiler_params=_params(("arbitrary",) * 3, _vmem_limit(3 * _nbytes((l2, rt, ct), BF16),
                                                                 2 * _nbytes((l2, rt, V7X_LANES), F32))),
        name="dft_pos_slabs",
    )(a.reshape(bsz, l2, l1, d), tc, ts)

    ct2 = _largest_block(d, 512, V7X_LANES)
    mat1 = pl.BlockSpec((l1, 2 * l1), lambda b, s, j: (0, 0))
    blk = pl.BlockSpec((1, 1, l1, ct2), lambda b, s, j: (b, s, 0, j))
    p, q = pl.pallas_call(
        _dft_dense_kernel,
        grid=(bsz, l2, d // ct2),
        in_specs=[blk, blk, mat1, mat1],
        out_specs=[blk, blk],
        out_shape=[slab_shape, slab_shape],
        compiler_params=_params(("arbitrary",) * 3, _vmem_limit(4 * _nbytes((l1, ct2), BF16),
                                                                 2 * _nbytes((l1, 2 * l1), BF16),
                                                                 scratch=4 * _nbytes((l1, ct2), F32))),
        name="dft_pos_dense",
    )(t_re, t_im, cs1, snc1)

    bm = _largest_block(m, 1024, 16)
    pq_spec = pl.BlockSpec((bm, cg), lambda i, g: (i, g))
    mat_c = pl.BlockSpec((2 * cg, cg), lambda i, g: (0, 0))
    y = pl.pallas_call(
        _chan_dft_kernel,
        grid=(m // bm, FOURIER_GROUPS),
        in_specs=[pq_spec, pq_spec, mat_c],
        out_specs=pq_spec,
        out_shape=jax.ShapeDtypeStruct((m, d), BF16),
        compiler_params=_params(("arbitrary",) * 2, _vmem_limit(3 * _nbytes((bm, cg), BF16),
                                                                 2 * _nbytes((cg, cg), BF16),
                                                                 scratch=2 * _nbytes((bm, cg), F32))),
        name="dft_channel",
    )(p.reshape(m, d), q.reshape(m, d), csn)
    return y.reshape(bsz, l2, l1, d).transpose(0, 2, 1, 3).reshape(m, d)


def _conv_aligned_span(width):
    first = CONV_HALO - (width - 1) // 2
    return (first + width - 1) // F32_SUBLANES * F32_SUBLANES


def _dwconv_kernel(prev_ref, cur_ref, next_ref, w_ref, b_ref, o_ref, win_sc, sh_sc, *, width, rows_chunk):
    i = pl.program_id(1)
    bt = cur_ref.shape[1]
    first = CONV_HALO - (width - 1) // 2
    win_sc[0:CONV_HALO] = jnp.where(i > 0, prev_ref[0], 0.0)
    win_sc[CONV_HALO:CONV_HALO + bt] = cur_ref[0]
    win_sc[CONV_HALO + bt:] = jnp.where(i < pl.num_programs(1) - 1, next_ref[0], 0.0)
    span = bt + _conv_aligned_span(width)
    for r in range(1, F32_SUBLANES):
        sh_sc[r - 1] = win_sc[r:r + span]
    for c in range(bt // rows_chunk):
        acc = None
        for k in range(width):
            r = (first + k) % F32_SUBLANES
            a = first + k - r + c * rows_chunk
            rows = win_sc[a:a + rows_chunk] if r == 0 else sh_sc[r - 1, a:a + rows_chunk]
            term = rows * w_ref[k:k + 1]
            acc = term if acc is None else acc + term
        o_ref[0, c * rows_chunk:(c + 1) * rows_chunk] = acc + b_ref[...]


def _depthwise_conv(u, w_dw, b_dw, bsz, length):
    m, c = u.shape
    width = w_dw.shape[0]
    assert (width - 1) // 2 <= CONV_HALO
    assert _conv_aligned_span(width) + F32_SUBLANES <= 2 * CONV_HALO
    bt = _largest_block(length, 128, CONV_HALO)
    cw = _largest_block(c, 512, V7X_LANES)
    rows_chunk = _largest_block(bt, 32, F32_SUBLANES)
    per_block = bt // CONV_HALO
    n_halo = length // CONV_HALO
    u3 = u.reshape(bsz, length, c)
    halo = lambda f: pl.BlockSpec((1, CONV_HALO, cw), f)
    out = pl.pallas_call(
        functools.partial(_dwconv_kernel, width=width, rows_chunk=rows_chunk),
        grid=(bsz, length // bt, c // cw),
        in_specs=[halo(lambda b, i, j: (b, jnp.maximum(i * per_block - 1, 0), j)),
                  pl.BlockSpec((1, bt, cw), lambda b, i, j: (b, i, j)),
                  halo(lambda b, i, j: (b, jnp.minimum((i + 1) * per_block, n_halo - 1), j)),
                  pl.BlockSpec((width, cw), lambda b, i, j: (0, j)),
                  pl.BlockSpec((1, cw), lambda b, i, j: (0, j))],
        out_specs=pl.BlockSpec((1, bt, cw), lambda b, i, j: (b, i, j)),
        out_shape=jax.ShapeDtypeStruct((bsz, length, c), F32),
        scratch_shapes=[pltpu.VMEM((bt + 2 * CONV_HALO, cw), F32),
                        pltpu.VMEM((F32_SUBLANES - 1, bt + _conv_aligned_span(width), cw), F32)],
        compiler_params=_params(("arbitrary",) * 3,
                                _vmem_limit(4 * _nbytes((bt, cw), F32),
                                            scratch=F32_SUBLANES * _nbytes((bt + 2 * CONV_HALO, cw), F32))),
        name="depthwise_conv",
    )(u3, u3, u3, w_dw, b_dw.reshape(1, c))
    return out.reshape(m, c)


def kernel(x, c, ctx, c_ctx, ada_down, ada_up, ada_b, norm_w, ffn_w_in, ffn_w_out, fourier_w, fourier_b,
           attn_w_qkv, attn_q_norm, attn_k_norm, attn_w_o, conv_w_pw1, conv_b_pw1, conv_w_dw, conv_b_dw,
           conv_ln_w, conv_ln_b, conv_w_pw2, conv_b_pw2, final_norm_w):
    bsz, seq, d = x.shape
    ctx_len = ctx.shape[1]
    depth = ada_down.shape[0]
    assert bsz + 1 <= MOD_ROWS and d % (FOURIER_GROUPS * V7X_LANES) == 0 and seq % GRID_W == 0
    assert seq % (16 * DFT_SLABS) == 0 and ctx_len % (16 * DFT_SLABS) == 0

    cond = jnp.zeros((MOD_ROWS, d), F32).at[:bsz].set(c).at[bsz].set(c_ctx)
    mod = _ada_modulation(cond, ada_down, ada_up, ada_b)

    lat = _Groups(0, seq)
    con = _Groups(bsz, None)
    s_lat = _prep_stream(x.reshape(bsz * seq, d))
    s_ctx = _prep_stream(ctx.reshape(bsz * ctx_len, d))
    rope = _rope_tables(seq)
    ffn_w_in, ffn_w_out, fourier_w, attn_w_qkv, attn_w_o, conv_w_pw1, conv_w_pw2 = (
        w.astype(BF16) for w in (ffn_w_in, ffn_w_out, fourier_w, attn_w_qkv, attn_w_o, conv_w_pw1, conv_w_pw2))

    for i in range(depth):
        kind = i % N_MIXERS
        j = i // N_MIXERS
        last = i == depth - 1
        ctx_in = (not last) or kind == 1
        ctx_out = not last
        m = mod[i]

        def ffn(stream, groups, s, which, emit_stream=True):
            norm = _Norm(norm_w[i, s], m[:, 3 * s], m[:, 3 * s + 1], groups)
            act = _gated_mm(stream, norm, _Weight(ffn_w_in, (i, which)), None, "swiglu", BF16)
            return _res_mm(act, _Weight(ffn_w_out, (i, which)), None, stream, m[:, 3 * s + 2], FFN_RES_WEIGHT,
                           groups, emit_stream)

        s_lat = ffn(s_lat, lat, 0, 0)
        if ctx_in:
            s_ctx = ffn(s_ctx, con, 0, 0)

        n_lat = _Norm(norm_w[i, 1], m[:, 3], m[:, 4], lat)
        n_ctx = _Norm(norm_w[i, 1], m[:, 3], m[:, 4], con)
        gate = m[:, 5]
        if kind == 0:
            w, b = _Weight(fourier_w, (j,)), fourier_b[j]
            y = _fourier_real_2d(_norm_apply(s_lat, n_lat), bsz, seq)
            s_lat = _res_mm(y, w, b, s_lat, gate, 1.0, lat)
            if ctx_out:
                y = _fourier_real_2d(_norm_apply(s_ctx, n_ctx), bsz, ctx_len)
                s_ctx = _res_mm(y, w, b, s_ctx, gate, 1.0, con)
        elif kind == 1:
            w_qkv, w_o = _Weight(attn_w_qkv, (j,)), _Weight(attn_w_o, (j,))
            kvw = (w_qkv.shape[1] - d) // 2
            qkv_x = _qkv_project(s_lat, n_lat, w_qkv, attn_q_norm[j], attn_k_norm[j], rope, seq)
            qkv_c = _qkv_project(s_ctx, n_ctx, w_qkv, attn_q_norm[j], attn_k_norm[j], None, ctx_len)
            qkv_x = qkv_x.reshape(bsz, seq, -1)
            qkv_c = qkv_c.reshape(bsz, ctx_len, -1)
            o_x = _attention(qkv_x, [qkv_x, qkv_c], d, kvw).reshape(bsz * seq, d)
            s_lat = _res_mm(o_x, w_o, None, s_lat, gate, 1.0, lat)
            if ctx_out:
                o_c = _attention(qkv_c, [qkv_c], d, kvw).reshape(bsz * ctx_len, d)
                s_ctx = _res_mm(o_c, w_o, None, s_ctx, gate, 1.0, con)
        else:
            w1, w2 = _Weight(conv_w_pw1, (j,)), _Weight(conv_w_pw2, (j,))

            def conv(stream, norm, length):
                u = _gated_mm(stream, norm, w1, conv_b_pw1[j], "glu", F32)
                u = _depthwise_conv(u, conv_w_dw[j], conv_b_dw[j], bsz, length)
                return _ln_silu(u, conv_ln_w[j], conv_ln_b[j])

            s_lat = _res_mm(conv(s_lat, n_lat, seq), w2, conv_b_pw2[j], s_lat, gate, 1.0, lat)
            if ctx_out:
                s_ctx = _res_mm(conv(s_ctx, n_ctx, ctx_len), w2, conv_b_pw2[j], s_ctx, gate, 1.0, con)

        s_lat = ffn(s_lat, lat, 2, 1, emit_stream=not last)
        if ctx_out:
            s_ctx = ffn(s_ctx, con, 2, 1)

    return _rms_norm(s_lat.x, final_norm_w).reshape(bsz, seq, d)
```

```python
import functools
import math
from typing import NamedTuple

import jax
import jax.numpy as jnp
import numpy as np
from jax import lax
from jax.experimental import pallas as pl
from jax.experimental.pallas import tpu as pltpu

F32 = jnp.float32
BF16 = jnp.bfloat16

HEAD_DIM = 128
GQA_GROUP = 4
GRID_W = 64
FOURIER_GROUPS = 4
N_MIXERS = 3
N_MOD = 9
FFN_RES_WEIGHT = 0.5
ROPE_THETA = 10000.0
EPS = 1e-6

V7X_VMEM_BYTES = 64 * 1024 * 1024
V7X_LANES = 128
F32_SUBLANES = 8
MOD_ROWS = 8
CONV_HALO = 16


def _vmem_limit(*block_bytes, scratch=0):
    need = 2 * sum(block_bytes) + scratch
    return int(min(V7X_VMEM_BYTES - (6 << 20), max(2 * need, 16 << 20)))


def _params(sem, vmem):
    return pltpu.CompilerParams(dimension_semantics=sem, vmem_limit_bytes=vmem)


def _largest_block(n, cap, mult):
    if n <= cap:
        return n
    b = (cap // mult) * mult
    while b >= mult:
        if n % b == 0:
            return b
        b -= mult
    raise ValueError(f"no block for {n} (cap {cap}, multiple {mult})")


def _nbytes(shape, dtype):
    return int(np.prod(shape)) * jnp.dtype(dtype).itemsize


def _silu(x):
    return x * jax.nn.sigmoid(x)


def _ada_down_kernel(c_ref, w_ref, o_ref):
    s = _silu(c_ref[...])
    o_ref[0] = jnp.dot(s.astype(BF16), w_ref[0].astype(BF16), preferred_element_type=F32)


def _ada_up_kernel(h_ref, w_ref, b_ref, o_ref):
    o_ref[0] = jnp.dot(h_ref[0].astype(BF16), w_ref[0].astype(BF16),
                       preferred_element_type=F32) + b_ref[0]


def _ada_modulation(cond, down, up, bias):
    depth, d, rank = down.shape
    nout = up.shape[2]
    bn = _largest_block(rank, 256, V7X_LANES)
    h = pl.pallas_call(
        _ada_down_kernel,
        grid=(depth, rank // bn),
        in_specs=[pl.BlockSpec((MOD_ROWS, d), lambda l, j: (0, 0)),
                  pl.BlockSpec((1, d, bn), lambda l, j: (l, 0, j))],
        out_specs=pl.BlockSpec((1, MOD_ROWS, bn), lambda l, j: (l, 0, j)),
        out_shape=jax.ShapeDtypeStruct((depth, MOD_ROWS, rank), F32),
        compiler_params=_params(("arbitrary", "arbitrary"),
                                _vmem_limit(_nbytes((d, bn), F32), _nbytes((MOD_ROWS, d), F32))),
        name="ada_down",
    )(cond, down)
    bn = _largest_block(nout, 2048, V7X_LANES)
    m = pl.pallas_call(
        _ada_up_kernel,
        grid=(depth, nout // bn),
        in_specs=[pl.BlockSpec((1, MOD_ROWS, rank), lambda l, j: (l, 0, 0)),
                  pl.BlockSpec((1, rank, bn), lambda l, j: (l, 0, j)),
                  pl.BlockSpec((1, 1, bn), lambda l, j: (l, 0, j))],
        out_specs=pl.BlockSpec((1, MOD_ROWS, bn), lambda l, j: (l, 0, j)),
        out_shape=jax.ShapeDtypeStruct((depth, MOD_ROWS, nout), F32),
        compiler_params=_params(("arbitrary", "arbitrary"), _vmem_limit(_nbytes((rank, bn), F32))),
        name="ada_up",
    )(h, up, bias.reshape(depth, 1, nout))
    return m.reshape(depth, MOD_ROWS, N_MOD, d)


class _Groups:
    def __init__(self, first, rows_per_group):
        self.first = first
        self.rows_per_group = rows_per_group

    def index(self, i, bm):
        if self.rows_per_group is None:
            return self.first
        return self.first + (i * bm) // self.rows_per_group

    def block_rows(self, m, cap):
        limit = m if self.rows_per_group is None else self.rows_per_group
        return _largest_block(limit, cap, F32_SUBLANES * 2)


class _Stream(NamedTuple):
    x: jax.Array
    xb: jax.Array
    sq: jax.Array


class _Norm(NamedTuple):
    w: jax.Array
    shift: jax.Array
    scale: jax.Array
    groups: _Groups

    def specs_and_args(self, bm, d):
        r = self.shift.shape[0]
        g = self.groups
        mod = pl.BlockSpec((1, 1, d), lambda i, *_: (g.index(i, bm), 0, 0))
        return ([pl.BlockSpec((1, d), lambda i, *_: (0, 0)), mod, mod],
                [self.w.reshape(1, d), self.shift.reshape(r, 1, d), self.scale.reshape(r, 1, d)])


NORM_ROWS = 32
NORM_COLS = 512


def _lane_partial_sq(x):
    sq = x * x
    part = sq[:, :V7X_LANES]
    for t in range(1, x.shape[1] // V7X_LANES):
        part = part + sq[:, t * V7X_LANES:(t + 1) * V7X_LANES]
    return part


def _normalize_into(h_ref, xb_ref, sq_ref, nw_ref, sh_ref, sc_ref):
    d = xb_ref.shape[1]
    cw = _largest_block(d, NORM_COLS, V7X_LANES)

    def chunk(r, carry):
        rows = pl.ds(pl.multiple_of(r * NORM_ROWS, NORM_ROWS), NORM_ROWS)
        rstd = lax.rsqrt(jnp.sum(sq_ref[rows, :], axis=-1, keepdims=True) / d + EPS)
        rstd = jnp.broadcast_to(rstd, (NORM_ROWS, cw))
        for c0 in range(0, d, cw):
            cols = slice(c0, c0 + cw)
            y = xb_ref[rows, cols].astype(F32) * rstd * nw_ref[:, cols]
            h_ref[rows, cols] = (y * (1.0 + sc_ref[0, :, cols]) + sh_ref[0, :, cols]).astype(h_ref.dtype)
        return carry

    lax.fori_loop(0, xb_ref.shape[0] // NORM_ROWS, chunk, 0)


def _prep_kernel(x_ref, xb_ref, sq_ref):
    x = x_ref[...]
    xb_ref[...] = x.astype(xb_ref.dtype)
    sq_ref[...] = _lane_partial_sq(x)


def _prep_stream(x):
    m, d = x.shape
    bm = _largest_block(m, 256, NORM_ROWS)
    row = lambda n: pl.BlockSpec((bm, n), lambda i: (i, 0))
    xb, sq = pl.pallas_call(
        _prep_kernel,
        grid=(m // bm,),
        in_specs=[row(d)],
        out_specs=[row(d), row(V7X_LANES)],
        out_shape=[jax.ShapeDtypeStruct((m, d), BF16), jax.ShapeDtypeStruct((m, V7X_LANES), F32)],
        compiler_params=_params(("arbitrary",), _vmem_limit(_nbytes((bm, d), F32), _nbytes((bm, d), BF16))),
        name="prep_stream",
    )(x)
    return _Stream(x, xb, sq)


def _norm_apply_kernel(xb_ref, sq_ref, nw_ref, sh_ref, sc_ref, o_ref):
    _normalize_into(o_ref, xb_ref, sq_ref, nw_ref, sh_ref, sc_ref)


def _norm_apply(stream, norm):
    m, d = stream.xb.shape
    bm = norm.groups.block_rows(m, 256)
    row = lambda n: pl.BlockSpec((bm, n), lambda i: (i, 0))
    nspecs, nargs = norm.specs_and_args(bm, d)
    return pl.pallas_call(
        _norm_apply_kernel,
        grid=(m // bm,),
        in_specs=[row(d), row(V7X_LANES)] + nspecs,
        out_specs=row(d),
        out_shape=jax.ShapeDtypeStruct((m, d), BF16),
        compiler_params=_params(("arbitrary",), _vmem_limit(2 * _nbytes((bm, d), BF16),
                                                            scratch=4 * _nbytes((NORM_ROWS, d), F32))),
        name="norm_apply",
    )(stream.xb, stream.sq, *nargs)


def _rms_norm_kernel(x_ref, nw_ref, o_ref):
    x = x_ref[...]
    o_ref[...] = x * lax.rsqrt(jnp.mean(x * x, axis=-1, keepdims=True) + EPS) * nw_ref[...]


def _rms_norm(x, norm_w):
    m, d = x.shape
    bm = _largest_block(m, 256, F32_SUBLANES)
    return pl.pallas_call(
        _rms_norm_kernel,
        grid=(m // bm,),
        in_specs=[pl.BlockSpec((bm, d), lambda i: (i, 0)), pl.BlockSpec((1, d), lambda i: (0, 0))],
        out_specs=pl.BlockSpec((bm, d), lambda i: (i, 0)),
        out_shape=jax.ShapeDtypeStruct((m, d), F32),
        compiler_params=_params(("arbitrary",), _vmem_limit(2 * _nbytes((bm, d), F32))),
        name="final_rms_norm",
    )(x, norm_w.reshape(1, d))


def _ln_silu_kernel(u_ref, w_ref, b_ref, o_ref):
    u = u_ref[...]
    xc = u - jnp.mean(u, axis=-1, keepdims=True)
    y = xc * lax.rsqrt(jnp.mean(xc * xc, axis=-1, keepdims=True) + EPS)
    o_ref[...] = _silu(y * w_ref[...] + b_ref[...]).astype(o_ref.dtype)


def _ln_silu(u, w, b):
    m, d = u.shape
    bm = _largest_block(m, 256, F32_SUBLANES * 2)
    vec = pl.BlockSpec((1, d), lambda i: (0, 0))
    return pl.pallas_call(
        _ln_silu_kernel,
        grid=(m // bm,),
        in_specs=[pl.BlockSpec((bm, d), lambda i: (i, 0)), vec, vec],
        out_specs=pl.BlockSpec((bm, d), lambda i: (i, 0)),
        out_shape=jax.ShapeDtypeStruct((m, d), BF16),
        compiler_params=_params(("arbitrary",), _vmem_limit(_nbytes((bm, d), F32), _nbytes((bm, d), BF16))),
        name="ln_silu",
    )(u, w.reshape(1, d), b.reshape(1, d))


class _Weight(NamedTuple):
    array: jax.Array
    lead: tuple

    @property
    def shape(self):
        return self.array.shape[len(self.lead):]

    def spec(self, bn, col):
        lead = self.lead
        k = self.shape[0]
        return pl.BlockSpec((None,) * len(lead) + (k, bn), lambda i, j: (*lead, 0, col(i, j)))


def _gated_mm_kernel(*refs, mode, has_bias):
    xb_ref, sq_ref, nw_ref, sh_ref, sc_ref, wa_ref, wb_ref = refs[:7]
    o_ref, h_sc = refs[-2:]

    @pl.when(pl.program_id(1) == 0)
    def _():
        _normalize_into(h_sc, xb_ref, sq_ref, nw_ref, sh_ref, sc_ref)

    a = h_sc[...]
    ga = jnp.dot(a, wa_ref[...], preferred_element_type=F32)
    gb = jnp.dot(a, wb_ref[...], preferred_element_type=F32)
    if has_bias:
        ga = ga + refs[7][...]
        gb = gb + refs[8][...]
    r = _silu(ga) * gb if mode == "swiglu" else ga * jax.nn.sigmoid(gb)
    o_ref[...] = r.astype(o_ref.dtype)


def _gated_mm(stream, norm, w, bias, mode, out_dtype):
    m, k = stream.xb.shape
    f = w.shape[1] // 2
    bm = norm.groups.block_rows(m, 1024)
    bn = _largest_block(f, 512, V7X_LANES)
    nb = f // bn
    nspecs, nargs = norm.specs_and_args(bm, k)
    in_specs = [pl.BlockSpec((bm, k), lambda i, j: (i, 0)),
                pl.BlockSpec((bm, V7X_LANES), lambda i, j: (i, 0))] + nspecs
    in_specs += [w.spec(bn, lambda i, j: j), w.spec(bn, lambda i, j: j + nb)]
    args = [stream.xb, stream.sq] + nargs + [w.array, w.array]
    if bias is not None:
        b2 = bias.reshape(1, 2 * f)
        in_specs += [pl.BlockSpec((1, bn), lambda i, j: (0, j)),
                     pl.BlockSpec((1, bn), lambda i, j: (0, j + nb))]
        args += [b2, b2]
    return pl.pallas_call(
        functools.partial(_gated_mm_kernel, mode=mode, has_bias=bias is not None),
        grid=(m // bm, nb),
        in_specs=in_specs,
        out_specs=pl.BlockSpec((bm, bn), lambda i, j: (i, j)),
        out_shape=jax.ShapeDtypeStruct((m, f), out_dtype),
        scratch_shapes=[pltpu.VMEM((bm, k), BF16)],
        compiler_params=_params(("arbitrary", "arbitrary"),
                                _vmem_limit(_nbytes((bm, k), BF16), 2 * _nbytes((k, bn), BF16),
                                            _nbytes((bm, bn), out_dtype),
                                            scratch=_nbytes((bm, k), BF16) + 3 * _nbytes((bm, bn), F32))),
        name=f"gated_mm_{mode}",
    )(*args)


def _res_mm_kernel(*refs, coef, has_bias, emit_stream):
    a_ref, w_ref = refs[:2]
    n_in = 5 if has_bias else 4
    x_ref, g_ref = refs[n_in - 2:n_in]
    o_ref = refs[n_in]
    y = jnp.dot(a_ref[...], w_ref[...], preferred_element_type=F32)
    if has_bias:
        y = y + refs[2][...]
    g = g_ref[0] if coef == 1.0 else coef * g_ref[0]
    xn = x_ref[...] + g * y
    o_ref[...] = xn
    if emit_stream:
        xb_ref, sq_ref = refs[n_in + 1:]
        xb_ref[...] = xn.astype(xb_ref.dtype)
        part = _lane_partial_sq(xn)
        j = pl.program_id(1)

        @pl.when(j == 0)
        def _():
            sq_ref[...] = part

        @pl.when(j > 0)
        def _():
            sq_ref[...] += part


def _res_mm(a, w, bias, stream, gate, coef, groups, emit_stream=True):
    x = stream.x
    m, k = a.shape
    n = w.shape[1]
    bm = groups.block_rows(m, 1024)
    bn = _largest_block(n, 512, V7X_LANES)
    r = gate.shape[0]
    in_specs = [pl.BlockSpec((bm, k), lambda i, j: (i, 0)), w.spec(bn, lambda i, j: j)]
    args = [a, w.array]
    if bias is not None:
        in_specs.append(pl.BlockSpec((1, bn), lambda i, j: (0, j)))
        args.append(bias.reshape(1, n))
    tile = pl.BlockSpec((bm, bn), lambda i, j: (i, j))
    in_specs += [tile, pl.BlockSpec((1, 1, bn), lambda i, j: (groups.index(i, bm), 0, j))]
    args += [x, gate.reshape(r, 1, n)]
    out_specs = [tile]
    out_shape = [jax.ShapeDtypeStruct((m, n), F32)]
    if emit_stream:
        out_specs += [tile, pl.BlockSpec((bm, V7X_LANES), lambda i, j: (i, 0))]
        out_shape += [jax.ShapeDtypeStruct((m, n), BF16), jax.ShapeDtypeStruct((m, V7X_LANES), F32)]
    out = pl.pallas_call(
        functools.partial(_res_mm_kernel, coef=coef, has_bias=bias is not None, emit_stream=emit_stream),
        grid=(m // bm, n // bn),
        in_specs=in_specs,
        out_specs=out_specs,
        out_shape=out_shape,
        compiler_params=_params(("arbitrary", "arbitrary"),
                                _vmem_limit(_nbytes((bm, k), BF16), _nbytes((k, bn), BF16),
                                            3 * _nbytes((bm, bn), F32), scratch=_nbytes((bm, bn), F32))),
        name="res_mm",
    )(*args)
    return _Stream(*out) if emit_stream else _Stream(out[0], None, None)


def _rope_tables(seq_len):
    n_rows = seq_len // GRID_W
    pairs = HEAD_DIM // 4
    row = jnp.broadcast_to(jnp.arange(n_rows, dtype=F32)[:, None], (n_rows, GRID_W)).reshape(-1)
    col = jnp.broadcast_to(jnp.arange(GRID_W, dtype=F32)[None, :], (n_rows, GRID_W)).reshape(-1)
    freqs = ROPE_THETA ** (-jnp.arange(pairs, dtype=F32) / pairs)
    ang = jnp.stack([row[:, None] * freqs, col[:, None] * freqs], axis=1)
    cos = jnp.cos(ang)
    sin = jnp.sin(ang)
    cos_t = jnp.stack([cos, cos], axis=2).reshape(seq_len, HEAD_DIM)
    sin_t = jnp.stack([-sin, sin], axis=2).reshape(seq_len, HEAD_DIM)
    return cos_t, sin_t


def _qkv_kernel(*refs, n_norm_blocks, rope):
    xb_ref, sq_ref, mw_ref, sh_ref, sc_ref, w_ref, nw_ref = refs[:7]
    if rope:
        cos_ref, sin_ref = refs[7:9]
    o_ref, h_sc = refs[-2:]
    j = pl.program_id(1)

    @pl.when(j == 0)
    def _():
        _normalize_into(h_sc, xb_ref, sq_ref, mw_ref, sh_ref, sc_ref)

    acc = jnp.dot(h_sc[...], w_ref[...], preferred_element_type=F32)
    bn = acc.shape[1]
    pairs = HEAD_DIM // 4

    @pl.when(j < n_norm_blocks)
    def _():
        if rope:
            cos = cos_ref[...]
            sin = sin_ref[...]
            lane = lax.broadcasted_iota(jnp.int32, (1, HEAD_DIM), 1)
            first_half = (lane % (2 * pairs)) < pairs
        for h in range(bn // HEAD_DIM):
            sl = slice(h * HEAD_DIM, (h + 1) * HEAD_DIM)
            t = acc[:, sl]
            y = t * lax.rsqrt(jnp.mean(t * t, axis=-1, keepdims=True) + EPS) * nw_ref[:, sl]
            if rope:
                partner = jnp.where(first_half, pltpu.roll(y, HEAD_DIM - pairs, 1), pltpu.roll(y, pairs, 1))
                y = y * cos + partner * sin
            o_ref[:, sl] = y.astype(o_ref.dtype)

    @pl.when(j >= n_norm_blocks)
    def _():
        o_ref[...] = acc.astype(o_ref.dtype)


def _qkv_project(stream, norm, w_qkv, q_norm, k_norm, rope_tables, seq_len):
    m, d = stream.xb.shape
    n = w_qkv.shape[1]
    kv_width = (n - d) // 2
    bm = _largest_block(seq_len, 1024, NORM_ROWS)
    bn = _largest_block(math.gcd(d, kv_width), 512, HEAD_DIM)
    q_scale = HEAD_DIM ** -0.5 * math.log2(math.e)
    nw = jnp.concatenate([jnp.tile(q_norm * q_scale, d // HEAD_DIM), jnp.tile(k_norm, kv_width // HEAD_DIM),
                          jnp.ones((kv_width,), F32)]).reshape(1, n)
    nspecs, nargs = norm.specs_and_args(bm, d)
    in_specs = [pl.BlockSpec((bm, d), lambda i, j: (i, 0)),
                pl.BlockSpec((bm, V7X_LANES), lambda i, j: (i, 0))] + nspecs
    in_specs += [w_qkv.spec(bn, lambda i, j: j), pl.BlockSpec((1, bn), lambda i, j: (0, j))]
    args = [stream.xb, stream.sq] + nargs + [w_qkv.array, nw]
    if rope_tables is not None:
        pos_blocks = seq_len // bm
        pos_spec = pl.BlockSpec((bm, HEAD_DIM), lambda i, j: (i % pos_blocks, 0))
        in_specs += [pos_spec, pos_spec]
        args += list(rope_tables)
    return pl.pallas_call(
        functools.partial(_qkv_kernel, n_norm_blocks=(d + kv_width) // bn, rope=rope_tables is not None),
        grid=(m // bm, n // bn),
        in_specs=in_specs,
        out_specs=pl.BlockSpec((bm, bn), lambda i, j: (i, j)),
        out_shape=jax.ShapeDtypeStruct((m, n), BF16),
        scratch_shapes=[pltpu.VMEM((bm, d), BF16)],
        compiler_params=_params(("arbitrary", "arbitrary"),
                                _vmem_limit(_nbytes((bm, d), BF16), _nbytes((d, bn), BF16), _nbytes((bm, bn), BF16),
                                            scratch=_nbytes((bm, d), BF16) + 2 * _nbytes((bm, bn), F32))),
        name="qkv_project",
    )(*args)


ATTN_KEY_CHUNK = 1024
ATTN_Q_ROWS = 128
ATTN_MAX_PARTS = 4


def _key_chunks(lengths):
    chunks, col = [], 0
    for src, length in enumerate(lengths):
        size = _largest_block(length, ATTN_KEY_CHUNK, V7X_LANES)
        for start in range(0, length, size):
            chunks.append((src, start, size, col))
            col += size
    return chunks


def _attn_kernel(*refs, n_src):
    q_ref, qn_ref = refs[:2]
    k_refs = refs[2:2 + n_src]
    v_refs = refs[2 + n_src:2 + 2 * n_src]
    o_ref, s_sc, m_sc = refs[2 + 2 * n_src:]
    bq = ATTN_Q_ROWS
    rows = GQA_GROUP * bq
    n_parts = q_ref.shape[1] // bq
    chunks = _key_chunks([k.shape[1] for k in k_refs])

    def q_rows(ref, part):
        return jnp.concatenate([ref[0, part * bq:(part + 1) * bq, g * HEAD_DIM:(g + 1) * HEAD_DIM]
                                for g in range(GQA_GROUP)], axis=0)

    def step(cur, m8_cur, q):
        nxt = 0 if cur is None else cur + 1
        m8 = None
        if cur is not None:
            l8 = jnp.zeros((F32_SUBLANES, rows), F32)
            acc = jnp.zeros((HEAD_DIM, rows), F32)
        for src, start, size, col in chunks:
            st = lax.dot_general(k_refs[src][0, start:start + size, :], q, (((1,), (1,)), ((), ())),
                                 preferred_element_type=F32)
            s_sc[nxt % 2, col:col + size, :] = st
            cm = jnp.max(st.reshape(size // F32_SUBLANES, F32_SUBLANES, rows), axis=0)
            m8 = cm if m8 is None else jnp.maximum(m8, cm)
            if cur is not None:
                s = s_sc[cur % 2, col:col + size, :].reshape(size // F32_SUBLANES, F32_SUBLANES, rows)
                p = jnp.exp2(s - m8_cur[None])
                l8 = l8 + jnp.sum(p, axis=0)
                pt = p.reshape(size, rows).astype(BF16)
                acc = acc + lax.dot_general(v_refs[src][0, start:start + size, :], pt, (((0,), (0,)), ((), ())),
                                            preferred_element_type=F32)
        if cur is not None:
            out_t = acc / jnp.sum(l8, axis=0, keepdims=True)
            for g in range(GQA_GROUP):
                o_ref[0, cur * bq:(cur + 1) * bq, g * HEAD_DIM:(g + 1) * HEAD_DIM] = (
                    out_t[:, g * bq:(g + 1) * bq].T.astype(o_ref.dtype))
        return jnp.broadcast_to(jnp.max(m8, axis=0, keepdims=True), (F32_SUBLANES, rows))

    @pl.when(pl.program_id(2) == 0)
    def _():
        m_sc[...] = step(None, None, q_rows(q_ref, 0))

    m8 = m_sc[...]
    for part in range(n_parts):
        q_next = q_rows(q_ref, part + 1) if part + 1 < n_parts else q_rows(qn_ref, 0)
        m8 = step(part, m8, q_next)
    m_sc[...] = m8


def _attention(q_src, kv_srcs, d, kv_width):
    b, lq, _ = q_src.shape
    n_kv = kv_width // HEAD_DIM
    gw = GQA_GROUP * HEAD_DIM
    assert lq % ATTN_Q_ROWS == 0
    bq = _largest_block(lq, ATTN_Q_ROWS * ATTN_MAX_PARTS, ATTN_Q_ROWS)
    rows = GQA_GROUP * ATTN_Q_ROWS
    k0 = d // HEAD_DIM
    v0 = (d + kv_width) // HEAD_DIM
    n_keys = sum(a.shape[1] for a in kv_srcs)
    n_parts = bq // ATTN_Q_ROWS
    assert n_parts % 2 == 0
    last_part = lq // ATTN_Q_ROWS - 1
    q_spec = pl.BlockSpec((1, bq, gw), lambda bi, hi, qi: (bi, qi, hi))
    qn_spec = pl.BlockSpec((1, ATTN_Q_ROWS, gw),
                           lambda bi, hi, qi: (bi, jnp.minimum((qi + 1) * n_parts, last_part), hi))
    k_specs = [pl.BlockSpec((1, a.shape[1], HEAD_DIM), lambda bi, hi, qi: (bi, 0, k0 + hi)) for a in kv_srcs]
    v_specs = [pl.BlockSpec((1, a.shape[1], HEAD_DIM), lambda bi, hi, qi: (bi, 0, v0 + hi)) for a in kv_srcs]
    return pl.pallas_call(
        functools.partial(_attn_kernel, n_src=len(kv_srcs)),
        grid=(b, n_kv, lq // bq),
        in_specs=[q_spec, qn_spec] + k_specs + v_specs,
        out_specs=pl.BlockSpec((1, bq, gw), lambda bi, hi, qi: (bi, qi, hi)),
        out_shape=jax.ShapeDtypeStruct((b, lq, d), BF16),
        scratch_shapes=[pltpu.VMEM((2, n_keys, rows), F32), pltpu.VMEM((F32_SUBLANES, rows), F32)],
        compiler_params=_params(("arbitrary", "arbitrary", "arbitrary"),
                                _vmem_limit(2 * _nbytes((bq, gw), BF16), 2 * _nbytes((n_keys, HEAD_DIM), BF16),
                                            scratch=2 * _nbytes((rows, n_keys), F32)
                                            + 2 * _nbytes((rows, ATTN_KEY_CHUNK), F32))),
        name="attention",
    )(q_src, q_src, *kv_srcs, *kv_srcs)


DFT_SLABS = 8


def _dft_tables(n):
    idx = np.arange(n, dtype=np.int64)
    ang = 2.0 * np.pi * ((idx[:, None] * idx[None, :]) % n) / n
    return np.cos(ang), np.sin(ang)


def _lincomb(coefs, xs):
    groups = {}
    for cf, x in zip(coefs, xs):
        mag = round(abs(float(cf)), 12)
        if mag == 0.0:
            continue
        pos, neg = groups.setdefault(mag, ([], []))
        (pos if cf > 0 else neg).append(x)
    total = None
    for mag, (pos, neg) in groups.items():
        term = functools.reduce(lambda a, b: a + b, pos) if pos else None
        if neg:
            nsum = functools.reduce(lambda a, b: a + b, neg)
            term = -nsum if term is None else term - nsum
        if mag != 1.0:
            term = term * mag
        total = term if total is None else total + term
    return total


def _dft_slab_kernel(a_ref, tc_ref, ts_ref, re_ref, im_ref):
    cos, sin = _dft_tables(DFT_SLABS)
    for lb in range(a_ref.shape[3] // V7X_LANES):
        sl = slice(lb * V7X_LANES, (lb + 1) * V7X_LANES)
        xs = [a_ref[0, n2, :, sl].astype(F32) for n2 in range(DFT_SLABS)]
        for k2 in range(DFT_SLABS):
            u = _lincomb(cos[k2], xs)
            v = _lincomb(sin[k2], xs)
            tc = tc_ref[k2]
            ts = ts_ref[k2]
            if v is None:
                re, im = u * tc, -(u * ts)
            else:
                re, im = u * tc - v * ts, -(v * tc) - u * ts
            re_ref[0, k2, :, sl] = re.astype(re_ref.dtype)
            im_ref[0, k2, :, sl] = im.astype(im_ref.dtype)


def _dft_dense_kernel(re_ref, im_ref, cs_ref, snc_ref, p_ref, q_ref):
    t = jnp.concatenate([re_ref[0, 0], im_ref[0, 0]], axis=0)
    p_ref[0, 0] = jnp.dot(cs_ref[...], t, preferred_element_type=F32).astype(p_ref.dtype)
    q_ref[0, 0] = jnp.dot(snc_ref[...], t, preferred_element_type=F32).astype(q_ref.dtype)


def _chan_dft_kernel(p_ref, q_ref, csn_ref, o_ref):
    pq = jnp.concatenate([p_ref[...], q_ref[...]], axis=1)
    o_ref[...] = jnp.dot(pq, csn_ref[...], preferred_element_type=F32).astype(o_ref.dtype)


def _fourier_real_2d(a, bsz, length):
    m, d = a.shape
    l2 = DFT_SLABS
    l1 = length // l2
    cg = d // FOURIER_GROUPS
    c1, s1 = _dft_tables(l1)
    cs1 = jnp.asarray(np.concatenate([c1, s1], axis=1), BF16)
    snc1 = jnp.asarray(np.concatenate([s1, -c1], axis=1), BF16)
    k2 = np.arange(l2, dtype=np.int64)[:, None]
    n1 = np.arange(l1, dtype=np.int64)[None, :]
    tw = 2.0 * np.pi * ((n1 * k2) % length) / length
    lanes = np.ones((1, 1, V7X_LANES))
    tc = jnp.asarray(np.cos(tw)[:, :, None] * lanes, F32)
    ts = jnp.asarray(np.sin(tw)[:, :, None] * lanes, F32)
    norm = 1.0 / math.sqrt(length * cg)
    cc, sc = _dft_tables(cg)
    csn = jnp.asarray(np.concatenate([cc, -sc], axis=0) * norm, BF16)

    rt = _largest_block(l1, 32, 16)
    ct = _largest_block(d, 1024, V7X_LANES)
    slab_spec = pl.BlockSpec((1, l2, rt, ct), lambda b, i, j: (b, 0, i, j))
    tw_spec = pl.BlockSpec((l2, rt, V7X_LANES), lambda b, i, j: (0, i, 0))
    slab_shape = jax.ShapeDtypeStruct((bsz, l2, l1, d), BF16)
    t_re, t_im = pl.pallas_call(
        _dft_slab_kernel,
        grid=(bsz, l1 // rt, d // ct),
        in_specs=[slab_spec, tw_spec, tw_spec],
        out_specs=[slab_spec, slab_spec],
        out_shape=[slab_shape, slab_shape],
        compiler_params=_params(("arbitrary",) * 3, _vmem_limit(3 * _nbytes((l2, rt, ct), BF16),
                                                                 2 * _nbytes((l2, rt, V7X_LANES), F32))),
        name="dft_pos_slabs",
    )(a.reshape(bsz, l2, l1, d), tc, ts)

    ct2 = _largest_block(d, 512, V7X_LANES)
    mat1 = pl.BlockSpec((l1, 2 * l1), lambda b, s, j: (0, 0))
    blk = pl.BlockSpec((1, 1, l1, ct2), lambda b, s, j: (b, s, 0, j))
    p, q = pl.pallas_call(
        _dft_dense_kernel,
        grid=(bsz, l2, d // ct2),
        in_specs=[blk, blk, mat1, mat1],
        out_specs=[blk, blk],
        out_shape=[slab_shape, slab_shape],
        compiler_params=_params(("arbitrary",) * 3, _vmem_limit(4 * _nbytes((l1, ct2), BF16),
                                                                 2 * _nbytes((l1, 2 * l1), BF16),
                                                                 scratch=4 * _nbytes((l1, ct2), F32))),
        name="dft_pos_dense",
    )(t_re, t_im, cs1, snc1)

    bm = _largest_block(m, 1024, 16)
    pq_spec = pl.BlockSpec((bm, cg), lambda i, g: (i, g))
    mat_c = pl.BlockSpec((2 * cg, cg), lambda i, g: (0, 0))
    y = pl.pallas_call(
        _chan_dft_kernel,
        grid=(m // bm, FOURIER_GROUPS),
        in_specs=[pq_spec, pq_spec, mat_c],
        out_specs=pq_spec,
        out_shape=jax.ShapeDtypeStruct((m, d), BF16),
        compiler_params=_params(("arbitrary",) * 2, _vmem_limit(3 * _nbytes((bm, cg), BF16),
                                                                 2 * _nbytes((cg, cg), BF16),
                                                                 scratch=2 * _nbytes((bm, cg), F32))),
        name="dft_channel",
    )(p.reshape(m, d), q.reshape(m, d), csn)
    return y.reshape(bsz, l2, l1, d).transpose(0, 2, 1, 3).reshape(m, d)


def _conv_aligned_span(width):
    first = CONV_HALO - (width - 1) // 2
    return (first + width - 1) // F32_SUBLANES * F32_SUBLANES


def _dwconv_kernel(prev_ref, cur_ref, next_ref, w_ref, b_ref, o_ref, win_sc, sh_sc, *, width, rows_chunk):
    i = pl.program_id(1)
    bt = cur_ref.shape[1]
    first = CONV_HALO - (width - 1) // 2
    win_sc[0:CONV_HALO] = jnp.where(i > 0, prev_ref[0], 0.0)
    win_sc[CONV_HALO:CONV_HALO + bt] = cur_ref[0]
    win_sc[CONV_HALO + bt:] = jnp.where(i < pl.num_programs(1) - 1, next_ref[0], 0.0)
    span = bt + _conv_aligned_span(width)
    for r in range(1, F32_SUBLANES):
        sh_sc[r - 1] = win_sc[r:r + span]
    for c in range(bt // rows_chunk):
        acc = None
        for k in range(width):
            r = (first + k) % F32_SUBLANES
            a = first + k - r + c * rows_chunk
            rows = win_sc[a:a + rows_chunk] if r == 0 else sh_sc[r - 1, a:a + rows_chunk]
            term = rows * w_ref[k:k + 1]
            acc = term if acc is None else acc + term
        o_ref[0, c * rows_chunk:(c + 1) * rows_chunk] = acc + b_ref[...]


def _depthwise_conv(u, w_dw, b_dw, bsz, length):
    m, c = u.shape
    width = w_dw.shape[0]
    assert (width - 1) // 2 <= CONV_HALO
    assert _conv_aligned_span(width) + F32_SUBLANES <= 2 * CONV_HALO
    bt = _largest_block(length, 128, CONV_HALO)
    cw = _largest_block(c, 512, V7X_LANES)
    rows_chunk = _largest_block(bt, 32, F32_SUBLANES)
    per_block = bt // CONV_HALO
    n_halo = length // CONV_HALO
    u3 = u.reshape(bsz, length, c)
    halo = lambda f: pl.BlockSpec((1, CONV_HALO, cw), f)
    out = pl.pallas_call(
        functools.partial(_dwconv_kernel, width=width, rows_chunk=rows_chunk),
        grid=(bsz, length // bt, c // cw),
        in_specs=[halo(lambda b, i, j: (b, jnp.maximum(i * per_block - 1, 0), j)),
                  pl.BlockSpec((1, bt, cw), lambda b, i, j: (b, i, j)),
                  halo(lambda b, i, j: (b, jnp.minimum((i + 1) * per_block, n_halo - 1), j)),
                  pl.BlockSpec((width, cw), lambda b, i, j: (0, j)),
                  pl.BlockSpec((1, cw), lambda b, i, j: (0, j))],
        out_specs=pl.BlockSpec((1, bt, cw), lambda b, i, j: (b, i, j)),
        out_shape=jax.ShapeDtypeStruct((bsz, length, c), F32),
        scratch_shapes=[pltpu.VMEM((bt + 2 * CONV_HALO, cw), F32),
                        pltpu.VMEM((F32_SUBLANES - 1, bt + _conv_aligned_span(width), cw), F32)],
        compiler_params=_params(("arbitrary",) * 3,
                                _vmem_limit(4 * _nbytes((bt, cw), F32),
                                            scratch=F32_SUBLANES * _nbytes((bt + 2 * CONV_HALO, cw), F32))),
        name="depthwise_conv",
    )(u3, u3, u3, w_dw, b_dw.reshape(1, c))
    return out.reshape(m, c)


def kernel(x, c, ctx, c_ctx, ada_down, ada_up, ada_b, norm_w, ffn_w_in, ffn_w_out, fourier_w, fourier_b,
           attn_w_qkv, attn_q_norm, attn_k_norm, attn_w_o, conv_w_pw1, conv_b_pw1, conv_w_dw, conv_b_dw,
           conv_ln_w, conv_ln_b, conv_w_pw2, conv_b_pw2, final_norm_w):
    bsz, seq, d = x.shape
    ctx_len = ctx.shape[1]
    depth = ada_down.shape[0]
    assert bsz + 1 <= MOD_ROWS and d % (FOURIER_GROUPS * V7X_LANES) == 0 and seq % GRID_W == 0
    assert seq % (16 * DFT_SLABS) == 0 and ctx_len % (16 * DFT_SLABS) == 0

    cond = jnp.zeros((MOD_ROWS, d), F32).at[:bsz].set(c).at[bsz].set(c_ctx)
    mod = _ada_modulation(cond, ada_down, ada_up, ada_b)

    lat = _Groups(0, seq)
    con = _Groups(bsz, None)
    s_lat = _prep_stream(x.reshape(bsz * seq, d))
    s_ctx = _prep_stream(ctx.reshape(bsz * ctx_len, d))
    rope = _rope_tables(seq)
    ffn_w_in, ffn_w_out, fourier_w, attn_w_qkv, attn_w_o, conv_w_pw1, conv_w_pw2 = (
        w.astype(BF16) for w in (ffn_w_in, ffn_w_out, fourier_w, attn_w_qkv, attn_w_o, conv_w_pw1, conv_w_pw2))

    for i in range(depth):
        kind = i % N_MIXERS
        j = i // N_MIXERS
        last = i == depth - 1
        ctx_in = (not last) or kind == 1
        ctx_out = not last
        m = mod[i]

        def ffn(stream, groups, s, which, emit_stream=True):
            norm = _Norm(norm_w[i, s], m[:, 3 * s], m[:, 3 * s + 1], groups)
            act = _gated_mm(stream, norm, _Weight(ffn_w_in, (i, which)), None, "swiglu", BF16)
            return _res_mm(act, _Weight(ffn_w_out, (i, which)), None, stream, m[:, 3 * s + 2], FFN_RES_WEIGHT,
                           groups, emit_stream)

        s_lat = ffn(s_lat, lat, 0, 0)
        if ctx_in:
            s_ctx = ffn(s_ctx, con, 0, 0)

        n_lat = _Norm(norm_w[i, 1], m[:, 3], m[:, 4], lat)
        n_ctx = _Norm(norm_w[i, 1], m[:, 3], m[:, 4], con)
        gate = m[:, 5]
        if kind == 0:
            w, b = _Weight(fourier_w, (j,)), fourier_b[j]
            y = _fourier_real_2d(_norm_apply(s_lat, n_lat), bsz, seq)
            s_lat = _res_mm(y, w, b, s_lat, gate, 1.0, lat)
            if ctx_out:
                y = _fourier_real_2d(_norm_apply(s_ctx, n_ctx), bsz, ctx_len)
                s_ctx = _res_mm(y, w, b, s_ctx, gate, 1.0, con)
        elif kind == 1:
            w_qkv, w_o = _Weight(attn_w_qkv, (j,)), _Weight(attn_w_o, (j,))
            kvw = (w_qkv.shape[1] - d) // 2
            qkv_x = _qkv_project(s_lat, n_lat, w_qkv, attn_q_norm[j], attn_k_norm[j], rope, seq)
            qkv_c = _qkv_project(s_ctx, n_ctx, w_qkv, attn_q_norm[j], attn_k_norm[j], None, ctx_len)
            qkv_x = qkv_x.reshape(bsz, seq, -1)
            qkv_c = qkv_c.reshape(bsz, ctx_len, -1)
            o_x = _attention(qkv_x, [qkv_x, qkv_c], d, kvw).reshape(bsz * seq, d)
            s_lat = _res_mm(o_x, w_o, None, s_lat, gate, 1.0, lat)
            if ctx_out:
                o_c = _attention(qkv_c, [qkv_c], d, kvw).reshape(bsz * ctx_len, d)
                s_ctx = _res_mm(o_c, w_o, None, s_ctx, gate, 1.0, con)
        else:
            w1, w2 = _Weight(conv_w_pw1, (j,)), _Weight(conv_w_pw2, (j,))

            def conv(stream, norm, length):
                u = _gated_mm(stream, norm, w1, conv_b_pw1[j], "glu", F32)
                u = _depthwise_conv(u, conv_w_dw[j], conv_b_dw[j], bsz, length)
                return _ln_silu(u, conv_ln_w[j], conv_ln_b[j])

            s_lat = _res_mm(conv(s_lat, n_lat, seq), w2, conv_b_pw2[j], s_lat, gate, 1.0, lat)
            if ctx_out:
                s_ctx = _res_mm(conv(s_ctx, n_ctx, ctx_len), w2, conv_b_pw2[j], s_ctx, gate, 1.0, con)

        s_lat = ffn(s_lat, lat, 2, 1, emit_stream=not last)
        if ctx_out:
            s_ctx = ffn(s_ctx, con, 2, 1)

    return _rms_norm(s_lat.x, final_norm_w).reshape(bsz, seq, d)
```

```python
import functools
import math
from typing import NamedTuple

import jax
import jax.numpy as jnp
import numpy as np
from jax import lax
from jax.experimental import pallas as pl
from jax.experimental.pallas import tpu as pltpu

F32 = jnp.float32
BF16 = jnp.bfloat16

HEAD_DIM = 128
GQA_GROUP = 4
GRID_W = 64
FOURIER_GROUPS = 4
N_MIXERS = 3
N_MOD = 9
FFN_RES_WEIGHT = 0.5
ROPE_THETA = 10000.0
EPS = 1e-6

V7X_VMEM_BYTES = 64 * 1024 * 1024
V7X_LANES = 128
F32_SUBLANES = 8
MOD_ROWS = 8
CONV_HALO = 16


def _vmem_limit(*block_bytes, scratch=0):
    need = 2 * sum(block_bytes) + scratch
    return int(min(V7X_VMEM_BYTES - (6 << 20), max(2 * need, 16 << 20)))


def _params(sem, vmem):
    return pltpu.CompilerParams(dimension_semantics=sem, vmem_limit_bytes=vmem)


def _largest_block(n, cap, mult):
    if n <= cap:
        return n
    b = (cap // mult) * mult
    while b >= mult:
        if n % b == 0:
            return b
        b -= mult
    raise ValueError(f"no block for {n} (cap {cap}, multiple {mult})")


def _nbytes(shape, dtype):
    return int(np.prod(shape)) * jnp.dtype(dtype).itemsize


def _silu(x):
    return x * jax.nn.sigmoid(x)


def _ada_down_kernel(c_ref, w_ref, o_ref):
    s = _silu(c_ref[...])
    o_ref[0] = jnp.dot(s.astype(BF16), w_ref[0].astype(BF16), preferred_element_type=F32)


def _ada_up_kernel(h_ref, w_ref, b_ref, o_ref):
    o_ref[0] = jnp.dot(h_ref[0].astype(BF16), w_ref[0].astype(BF16),
                       preferred_element_type=F32) + b_ref[0]


def _ada_modulation(cond, down, up, bias):
    depth, d, rank = down.shape
    nout = up.shape[2]
    bn = _largest_block(rank, 256, V7X_LANES)
    h = pl.pallas_call(
        _ada_down_kernel,
        grid=(depth, rank // bn),
        in_specs=[pl.BlockSpec((MOD_ROWS, d), lambda l, j: (0, 0)),
                  pl.BlockSpec((1, d, bn), lambda l, j: (l, 0, j))],
        out_specs=pl.BlockSpec((1, MOD_ROWS, bn), lambda l, j: (l, 0, j)),
        out_shape=jax.ShapeDtypeStruct((depth, MOD_ROWS, rank), F32),
        compiler_params=_params(("arbitrary", "arbitrary"),
                                _vmem_limit(_nbytes((d, bn), F32), _nbytes((MOD_ROWS, d), F32))),
        name="ada_down",
    )(cond, down)
    bn = _largest_block(nout, 2048, V7X_LANES)
    m = pl.pallas_call(
        _ada_up_kernel,
        grid=(depth, nout // bn),
        in_specs=[pl.BlockSpec((1, MOD_ROWS, rank), lambda l, j: (l, 0, 0)),
                  pl.BlockSpec((1, rank, bn), lambda l, j: (l, 0, j)),
                  pl.BlockSpec((1, 1, bn), lambda l, j: (l, 0, j))],
        out_specs=pl.BlockSpec((1, MOD_ROWS, bn), lambda l, j: (l, 0, j)),
        out_shape=jax.ShapeDtypeStruct((depth, MOD_ROWS, nout), F32),
        compiler_params=_params(("arbitrary", "arbitrary"), _vmem_limit(_nbytes((rank, bn), F32))),
        name="ada_up",
    )(h, up, bias.reshape(depth, 1, nout))
    return m.reshape(depth, MOD_ROWS, N_MOD, d)


class _Groups:
    def __init__(self, first, rows_per_group):
        self.first = first
        self.rows_per_group = rows_per_group

    def index(self, i, bm):
        if self.rows_per_group is None:
            return self.first
        return self.first + (i * bm) // self.rows_per_group

    def block_rows(self, m, cap):
        limit = m if self.rows_per_group is None else self.rows_per_group
        return _largest_block(limit, cap, F32_SUBLANES * 2)


class _Norm(NamedTuple):
    w: jax.Array
    shift: jax.Array
    scale: jax.Array
    groups: _Groups

    def gain_specs_and_args(self, bm, bn, col):
        r, d = self.scale.shape
        g = self.groups
        return ([pl.BlockSpec((1, bn), lambda *idx: (0, col(*idx))),
                 pl.BlockSpec((1, 1, bn), lambda *idx: (g.index(idx[0], bm), 0, col(*idx)))],
                [self.w.reshape(1, d), self.scale.reshape(r, 1, d)])


class _Stream(NamedTuple):
    x: jax.Array
    xg: jax.Array
    sq: jax.Array


def _lane_partial_sq(x):
    sq = x * x
    part = sq[:, :V7X_LANES]
    for t in range(1, x.shape[1] // V7X_LANES):
        part = part + sq[:, t * V7X_LANES:(t + 1) * V7X_LANES]
    return part


def _row_rstd(sq_ref, d):
    ms = jnp.sum(sq_ref[...], axis=-1, keepdims=True) / d
    return jnp.broadcast_to(lax.rsqrt(ms + EPS), sq_ref.shape)


def _scale_rows(acc, rstd):
    return jnp.concatenate([acc[:, t * V7X_LANES:(t + 1) * V7X_LANES] * rstd
                            for t in range(acc.shape[1] // V7X_LANES)], axis=1)


def _prep_kernel(x_ref, nw_ref, sc_ref, xg_ref, sq_ref):
    x = x_ref[...]
    xg_ref[...] = (x * (nw_ref[...] * (1.0 + sc_ref[0]))).astype(xg_ref.dtype)
    sq_ref[...] = _lane_partial_sq(x)


def _prep_stream(x, norm):
    m, d = x.shape
    bm = norm.groups.block_rows(m, 256)
    row = lambda n: pl.BlockSpec((bm, n), lambda i: (i, 0))
    gspecs, gargs = norm.gain_specs_and_args(bm, d, lambda i: 0)
    xg, sq = pl.pallas_call(
        _prep_kernel,
        grid=(m // bm,),
        in_specs=[row(d)] + gspecs,
        out_specs=[row(d), row(V7X_LANES)],
        out_shape=[jax.ShapeDtypeStruct((m, d), BF16), jax.ShapeDtypeStruct((m, V7X_LANES), F32)],
        compiler_params=_params(("arbitrary",), _vmem_limit(_nbytes((bm, d), F32), _nbytes((bm, d), BF16))),
        name="prep_stream",
    )(x, *gargs)
    return _Stream(x, xg, sq)


def _norm_apply_kernel(xg_ref, sq_ref, sh_ref, o_ref):
    d = xg_ref.shape[1]
    rstd = lax.rsqrt(jnp.sum(sq_ref[...], axis=-1, keepdims=True) / d + EPS)
    o_ref[...] = (xg_ref[...].astype(F32) * rstd + sh_ref[0]).astype(o_ref.dtype)


def _norm_apply(stream, norm):
    m, d = stream.xg.shape
    bm = norm.groups.block_rows(m, 256)
    r = norm.shift.shape[0]
    g = norm.groups
    row = lambda n: pl.BlockSpec((bm, n), lambda i: (i, 0))
    return pl.pallas_call(
        _norm_apply_kernel,
        grid=(m // bm,),
        in_specs=[row(d), row(V7X_LANES), pl.BlockSpec((1, 1, d), lambda i: (g.index(i, bm), 0, 0))],
        out_specs=row(d),
        out_shape=jax.ShapeDtypeStruct((m, d), BF16),
        compiler_params=_params(("arbitrary",), _vmem_limit(2 * _nbytes((bm, d), BF16),
                                                            scratch=2 * _nbytes((bm, d), F32))),
        name="norm_apply",
    )(stream.xg, stream.sq, norm.shift.reshape(r, 1, d))


def _rms_norm_kernel(x_ref, nw_ref, o_ref):
    x = x_ref[...]
    o_ref[...] = x * lax.rsqrt(jnp.mean(x * x, axis=-1, keepdims=True) + EPS) * nw_ref[...]


def _rms_norm(x, norm_w):
    m, d = x.shape
    bm = _largest_block(m, 256, F32_SUBLANES)
    return pl.pallas_call(
        _rms_norm_kernel,
        grid=(m // bm,),
        in_specs=[pl.BlockSpec((bm, d), lambda i: (i, 0)), pl.BlockSpec((1, d), lambda i: (0, 0))],
        out_specs=pl.BlockSpec((bm, d), lambda i: (i, 0)),
        out_shape=jax.ShapeDtypeStruct((m, d), F32),
        compiler_params=_params(("arbitrary",), _vmem_limit(2 * _nbytes((bm, d), F32))),
        name="final_rms_norm",
    )(x, norm_w.reshape(1, d))


def _ln_silu_kernel(u_ref, w_ref, b_ref, o_ref):
    u = u_ref[...]
    xc = u - jnp.mean(u, axis=-1, keepdims=True)
    y = xc * lax.rsqrt(jnp.mean(xc * xc, axis=-1, keepdims=True) + EPS)
    o_ref[...] = _silu(y * w_ref[...] + b_ref[...]).astype(o_ref.dtype)


def _ln_silu(u, w, b):
    m, d = u.shape
    bm = _largest_block(m, 256, F32_SUBLANES * 2)
    vec = pl.BlockSpec((1, d), lambda i: (0, 0))
    return pl.pallas_call(
        _ln_silu_kernel,
        grid=(m // bm,),
        in_specs=[pl.BlockSpec((bm, d), lambda i: (i, 0)), vec, vec],
        out_specs=pl.BlockSpec((bm, d), lambda i: (i, 0)),
        out_shape=jax.ShapeDtypeStruct((m, d), BF16),
        compiler_params=_params(("arbitrary",), _vmem_limit(_nbytes((bm, d), F32), _nbytes((bm, d), BF16))),
        name="ln_silu",
    )(u, w.reshape(1, d), b.reshape(1, d))


class _Weight(NamedTuple):
    array: jax.Array
    lead: tuple

    @property
    def shape(self):
        return self.array.shape[len(self.lead):]

    def spec(self, bn, col):
        lead = self.lead
        k = self.shape[0]
        return pl.BlockSpec((None,) * len(lead) + (k, bn), lambda *idx: (*lead, 0, col(*idx)))


def _shift_mm_kernel(*refs, has_bias):
    s_ref, w_ref = refs[:2]
    o_ref = refs[-1]
    y = jnp.dot(s_ref[...].astype(BF16), w_ref[...], preferred_element_type=F32)
    o_ref[...] = y + refs[2][...] if has_bias else y


def _shift_mm(norm, w, bias):
    r, k = norm.shift.shape
    n = w.shape[1]
    bn = _largest_block(n, 1024, V7X_LANES)
    in_specs = [pl.BlockSpec((r, k), lambda j: (0, 0)), w.spec(bn, lambda j: j)]
    args = [norm.shift, w.array]
    if bias is not None:
        in_specs.append(pl.BlockSpec((1, bn), lambda j: (0, j)))
        args.append(bias.reshape(1, n))
    out = pl.pallas_call(
        functools.partial(_shift_mm_kernel, has_bias=bias is not None),
        grid=(n // bn,),
        in_specs=in_specs,
        out_specs=pl.BlockSpec((r, bn), lambda j: (0, j)),
        out_shape=jax.ShapeDtypeStruct((r, n), F32),
        compiler_params=_params(("arbitrary",), _vmem_limit(_nbytes((k, bn), BF16), _nbytes((r, k), F32))),
        name="shift_mm",
    )(*args)
    return out.reshape(r, 1, n)


def _gated_mm_kernel(xg_ref, sq_ref, wa_ref, wb_ref, ca_ref, cb_ref, o_ref, rstd_sc, *, mode):
    @pl.when(pl.program_id(1) == 0)
    def _():
        rstd_sc[...] = _row_rstd(sq_ref, xg_ref.shape[1])

    a = xg_ref[...]
    rstd = rstd_sc[...]
    ga = _scale_rows(jnp.dot(a, wa_ref[...], preferred_element_type=F32), rstd) + ca_ref[0]
    gb = _scale_rows(jnp.dot(a, wb_ref[...], preferred_element_type=F32), rstd) + cb_ref[0]
    r = _silu(ga) * gb if mode == "swiglu" else ga * jax.nn.sigmoid(gb)
    o_ref[...] = r.astype(o_ref.dtype)


def _gated_mm(stream, groups, w, c, mode, out_dtype):
    m, k = stream.xg.shape
    f = w.shape[1] // 2
    bm = groups.block_rows(m, 1024)
    bn = _largest_block(f, 512, V7X_LANES)
    nb = f // bn
    g = groups
    add = lambda off: pl.BlockSpec((1, 1, bn), lambda i, j: (g.index(i, bm), 0, j + off))
    return pl.pallas_call(
        functools.partial(_gated_mm_kernel, mode=mode),
        grid=(m // bm, nb),
        in_specs=[pl.BlockSpec((bm, k), lambda i, j: (i, 0)),
                  pl.BlockSpec((bm, V7X_LANES), lambda i, j: (i, 0)),
                  w.spec(bn, lambda i, j: j), w.spec(bn, lambda i, j: j + nb), add(0), add(nb)],
        out_specs=pl.BlockSpec((bm, bn), lambda i, j: (i, j)),
        out_shape=jax.ShapeDtypeStruct((m, f), out_dtype),
        scratch_shapes=[pltpu.VMEM((bm, V7X_LANES), F32)],
        compiler_params=_params(("arbitrary", "arbitrary"),
                                _vmem_limit(_nbytes((bm, k), BF16), 2 * _nbytes((k, bn), BF16),
                                            _nbytes((bm, bn), out_dtype), scratch=4 * _nbytes((bm, bn), F32))),
        name=f"gated_mm_{mode}",
    )(stream.xg, stream.sq, w.array, w.array, c, c)


def _res_mm_kernel(*refs, coef, has_bias, emit_stream):
    a_ref, w_ref = refs[:2]
    n_in = 5 if has_bias else 4
    x_ref, g_ref = refs[n_in - 2:n_in]
    y = jnp.dot(a_ref[...], w_ref[...], preferred_element_type=F32)
    if has_bias:
        y = y + refs[2][...]
    g = g_ref[0] if coef == 1.0 else coef * g_ref[0]
    xn = x_ref[...] + g * y
    if not emit_stream:
        refs[n_in][...] = xn
        return
    nw_ref, sc_ref, o_ref, xg_ref, sq_ref = refs[n_in:]
    o_ref[...] = xn
    xg_ref[...] = (xn * (nw_ref[...] * (1.0 + sc_ref[0]))).astype(xg_ref.dtype)
    part = _lane_partial_sq(xn)
    j = pl.program_id(1)

    @pl.when(j == 0)
    def _():
        sq_ref[...] = part

    @pl.when(j > 0)
    def _():
        sq_ref[...] += part


def _res_mm(a, w, bias, stream, gate, coef, groups, next_norm):
    x = stream.x
    m, k = a.shape
    n = w.shape[1]
    bm = groups.block_rows(m, 1024)
    bn = _largest_block(n, 512, V7X_LANES)
    r = gate.shape[0]
    in_specs = [pl.BlockSpec((bm, k), lambda i, j: (i, 0)), w.spec(bn, lambda i, j: j)]
    args = [a, w.array]
    if bias is not None:
        in_specs.append(pl.BlockSpec((1, bn), lambda i, j: (0, j)))
        args.append(bias.reshape(1, n))
    tile = pl.BlockSpec((bm, bn), lambda i, j: (i, j))
    in_specs += [tile, pl.BlockSpec((1, 1, bn), lambda i, j: (groups.index(i, bm), 0, j))]
    args += [x, gate.reshape(r, 1, n)]
    out_specs = [tile]
    out_shape = [jax.ShapeDtypeStruct((m, n), F32)]
    if next_norm is not None:
        gspecs, gargs = next_norm.gain_specs_and_args(bm, bn, lambda i, j: j)
        in_specs += gspecs
        args += gargs
        out_specs += [tile, pl.BlockSpec((bm, V7X_LANES), lambda i, j: (i, 0))]
        out_shape += [jax.ShapeDtypeStruct((m, n), BF16), jax.ShapeDtypeStruct((m, V7X_LANES), F32)]
    out = pl.pallas_call(
        functools.partial(_res_mm_kernel, coef=coef, has_bias=bias is not None, emit_stream=next_norm is not None),
        grid=(m // bm, n // bn),
        in_specs=in_specs,
        out_specs=out_specs,
        out_shape=out_shape,
        compiler_params=_params(("arbitrary", "arbitrary"),
                                _vmem_limit(_nbytes((bm, k), BF16), _nbytes((k, bn), BF16),
                                            3 * _nbytes((bm, bn), F32), scratch=_nbytes((bm, bn), F32))),
        name="res_mm",
    )(*args)
    return _Stream(*out) if next_norm is not None else _Stream(out[0], None, None)


def _rope_tables(seq_len):
    n_rows = seq_len // GRID_W
    pairs = HEAD_DIM // 4
    row = jnp.broadcast_to(jnp.arange(n_rows, dtype=F32)[:, None], (n_rows, GRID_W)).reshape(-1)
    col = jnp.broadcast_to(jnp.arange(GRID_W, dtype=F32)[None, :], (n_rows, GRID_W)).reshape(-1)
    freqs = ROPE_THETA ** (-jnp.arange(pairs, dtype=F32) / pairs)
    ang = jnp.stack([row[:, None] * freqs, col[:, None] * freqs], axis=1)
    cos = jnp.cos(ang)
    sin = jnp.sin(ang)
    cos_t = jnp.stack([cos, cos], axis=2).reshape(seq_len, HEAD_DIM)
    sin_t = jnp.stack([-sin, sin], axis=2).reshape(seq_len, HEAD_DIM)
    return cos_t, sin_t


def _qkv_kernel(*refs, n_norm_blocks, rope):
    xg_ref, sq_ref, w_ref, c_ref, nw_ref = refs[:5]
    if rope:
        cos_ref, sin_ref = refs[5:7]
    o_ref, rstd_sc = refs[-2:]
    j = pl.program_id(1)

    @pl.when(j == 0)
    def _():
        rstd_sc[...] = _row_rstd(sq_ref, xg_ref.shape[1])

    acc = _scale_rows(jnp.dot(xg_ref[...], w_ref[...], preferred_element_type=F32), rstd_sc[...]) + c_ref[0]
    bn = acc.shape[1]
    pairs = HEAD_DIM // 4

    @pl.when(j < n_norm_blocks)
    def _():
        if rope:
            cos = cos_ref[...]
            sin = sin_ref[...]
            lane = lax.broadcasted_iota(jnp.int32, (1, HEAD_DIM), 1)
            first_half = (lane % (2 * pairs)) < pairs
        for h in range(bn // HEAD_DIM):
            sl = slice(h * HEAD_DIM, (h + 1) * HEAD_DIM)
            t = acc[:, sl]
            y = t * lax.rsqrt(jnp.mean(t * t, axis=-1, keepdims=True) + EPS) * nw_ref[:, sl]
            if rope:
                partner = jnp.where(first_half, pltpu.roll(y, HEAD_DIM - pairs, 1), pltpu.roll(y, pairs, 1))
                y = y * cos + partner * sin
            o_ref[:, sl] = y.astype(o_ref.dtype)

    @pl.when(j >= n_norm_blocks)
    def _():
        o_ref[...] = acc.astype(o_ref.dtype)


def _qkv_project(stream, groups, w_qkv, c, q_norm, k_norm, rope_tables, seq_len):
    m, d = stream.xg.shape
    n = w_qkv.shape[1]
    kv_width = (n - d) // 2
    bm = _largest_block(seq_len, 1024, 16)
    bn = _largest_block(math.gcd(d, kv_width), 512, HEAD_DIM)
    q_scale = HEAD_DIM ** -0.5 * math.log2(math.e)
    nw = jnp.concatenate([jnp.tile(q_norm * q_scale, d // HEAD_DIM), jnp.tile(k_norm, kv_width // HEAD_DIM),
                          jnp.ones((kv_width,), F32)]).reshape(1, n)
    g = groups
    in_specs = [pl.BlockSpec((bm, d), lambda i, j: (i, 0)),
                pl.BlockSpec((bm, V7X_LANES), lambda i, j: (i, 0)),
                w_qkv.spec(bn, lambda i, j: j),
                pl.BlockSpec((1, 1, bn), lambda i, j: (g.index(i, bm), 0, j)),
                pl.BlockSpec((1, bn), lambda i, j: (0, j))]
    args = [stream.xg, stream.sq, w_qkv.array, c, nw]
    if rope_tables is not None:
        pos_blocks = seq_len // bm
        pos_spec = pl.BlockSpec((bm, HEAD_DIM), lambda i, j: (i % pos_blocks, 0))
        in_specs += [pos_spec, pos_spec]
        args += list(rope_tables)
    return pl.pallas_call(
        functools.partial(_qkv_kernel, n_norm_blocks=(d + kv_width) // bn, rope=rope_tables is not None),
        grid=(m // bm, n // bn),
        in_specs=in_specs,
        out_specs=pl.BlockSpec((bm, bn), lambda i, j: (i, j)),
        out_shape=jax.ShapeDtypeStruct((m, n), BF16),
        scratch_shapes=[pltpu.VMEM((bm, V7X_LANES), F32)],
        compiler_params=_params(("arbitrary", "arbitrary"),
                                _vmem_limit(_nbytes((bm, d), BF16), _nbytes((d, bn), BF16), _nbytes((bm, bn), BF16),
                                            scratch=3 * _nbytes((bm, bn), F32))),
        name="qkv_project",
    )(*args)


ATTN_KEY_CHUNK = 1024
ATTN_Q_ROWS = 128
ATTN_MAX_PARTS = 4


def _key_chunks(lengths):
    chunks, col = [], 0
    for src, length in enumerate(lengths):
        size = _largest_block(length, ATTN_KEY_CHUNK, V7X_LANES)
        for start in range(0, length, size):
            chunks.append((src, start, size, col))
            col += size
    return chunks


def _attn_kernel(*refs, n_src):
    q_ref, qn_ref = refs[:2]
    k_refs = refs[2:2 + n_src]
    v_refs = refs[2 + n_src:2 + 2 * n_src]
    o_ref, s_sc, m_sc = refs[2 + 2 * n_src:]
    bq = ATTN_Q_ROWS
    rows = GQA_GROUP * bq
    n_parts = q_ref.shape[1] // bq
    chunks = _key_chunks([k.shape[1] for k in k_refs])

    def q_rows(ref, part):
        return jnp.concatenate([ref[0, part * bq:(part + 1) * bq, g * HEAD_DIM:(g + 1) * HEAD_DIM]
                                for g in range(GQA_GROUP)], axis=0)

    def step(cur, m8_cur, q):
        nxt = 0 if cur is None else cur + 1
        m8 = None
        if cur is not None:
            l8 = jnp.zeros((F32_SUBLANES, rows), F32)
            acc = jnp.zeros((HEAD_DIM, rows), F32)
        for src, start, size, col in chunks:
            st = lax.dot_general(k_refs[src][0, start:start + size, :], q, (((1,), (1,)), ((), ())),
                                 preferred_element_type=F32)
            s_sc[nxt % 2, col:col + size, :] = st
            cm = jnp.max(st.reshape(size // F32_SUBLANES, F32_SUBLANES, rows), axis=0)
            m8 = cm if m8 is None else jnp.maximum(m8, cm)
            if cur is not None:
                s = s_sc[cur % 2, col:col + size, :].reshape(size // F32_SUBLANES, F32_SUBLANES, rows)
                p = jnp.exp2(s - m8_cur[None])
                l8 = l8 + jnp.sum(p, axis=0)
                pt = p.reshape(size, rows).astype(BF16)
                acc = acc + lax.dot_general(v_refs[src][0, start:start + size, :], pt, (((0,), (0,)), ((), ())),
                                            preferred_element_type=F32)
        if cur is not None:
            out_t = acc / jnp.sum(l8, axis=0, keepdims=True)
            for g in range(GQA_GROUP):
                o_ref[0, cur * bq:(cur + 1) * bq, g * HEAD_DIM:(g + 1) * HEAD_DIM] = (
                    out_t[:, g * bq:(g + 1) * bq].T.astype(o_ref.dtype))
        return jnp.broadcast_to(jnp.max(m8, axis=0, keepdims=True), (F32_SUBLANES, rows))

    @pl.when(pl.program_id(2) == 0)
    def _():
        m_sc[...] = step(None, None, q_rows(q_ref, 0))

    m8 = m_sc[...]
    for part in range(n_parts):
        q_next = q_rows(q_ref, part + 1) if part + 1 < n_parts else q_rows(qn_ref, 0)
        m8 = step(part, m8, q_next)
    m_sc[...] = m8


def _attention(q_src, kv_srcs, d, kv_width):
    b, lq, _ = q_src.shape
    n_kv = kv_width // HEAD_DIM
    gw = GQA_GROUP * HEAD_DIM
    assert lq % ATTN_Q_ROWS == 0
    bq = _largest_block(lq, ATTN_Q_ROWS * ATTN_MAX_PARTS, ATTN_Q_ROWS)
    rows = GQA_GROUP * ATTN_Q_ROWS
    k0 = d // HEAD_DIM
    v0 = (d + kv_width) // HEAD_DIM
    n_keys = sum(a.shape[1] for a in kv_srcs)
    n_parts = bq // ATTN_Q_ROWS
    assert n_parts % 2 == 0
    last_part = lq // ATTN_Q_ROWS - 1
    q_spec = pl.BlockSpec((1, bq, gw), lambda bi, hi, qi: (bi, qi, hi))
    qn_spec = pl.BlockSpec((1, ATTN_Q_ROWS, gw),
                           lambda bi, hi, qi: (bi, jnp.minimum((qi + 1) * n_parts, last_part), hi))
    k_specs = [pl.BlockSpec((1, a.shape[1], HEAD_DIM), lambda bi, hi, qi: (bi, 0, k0 + hi)) for a in kv_srcs]
    v_specs = [pl.BlockSpec((1, a.shape[1], HEAD_DIM), lambda bi, hi, qi: (bi, 0, v0 + hi)) for a in kv_srcs]
    return pl.pallas_call(
        functools.partial(_attn_kernel, n_src=len(kv_srcs)),
        grid=(b, n_kv, lq // bq),
        in_specs=[q_spec, qn_spec] + k_specs + v_specs,
        out_specs=pl.BlockSpec((1, bq, gw), lambda bi, hi, qi: (bi, qi, hi)),
        out_shape=jax.ShapeDtypeStruct((b, lq, d), BF16),
        scratch_shapes=[pltpu.VMEM((2, n_keys, rows), F32), pltpu.VMEM((F32_SUBLANES, rows), F32)],
        compiler_params=_params(("arbitrary", "arbitrary", "arbitrary"),
                                _vmem_limit(2 * _nbytes((bq, gw), BF16), 2 * _nbytes((n_keys, HEAD_DIM), BF16),
                                            scratch=2 * _nbytes((rows, n_keys), F32)
                                            + 2 * _nbytes((rows, ATTN_KEY_CHUNK), F32))),
        name="attention",
    )(q_src, q_src, *kv_srcs, *kv_srcs)


DFT_SLABS = 8


def _dft_tables(n):
    idx = np.arange(n, dtype=np.int64)
    ang = 2.0 * np.pi * ((idx[:, None] * idx[None, :]) % n) / n
    return np.cos(ang), np.sin(ang)


def _lincomb(coefs, xs):
    groups = {}
    for cf, x in zip(coefs, xs):
        mag = round(abs(float(cf)), 12)
        if mag == 0.0:
            continue
        pos, neg = groups.setdefault(mag, ([], []))
        (pos if cf > 0 else neg).append(x)
    total = None
    for mag, (pos, neg) in groups.items():
        term = functools.reduce(lambda a, b: a + b, pos) if pos else None
        if neg:
            nsum = functools.reduce(lambda a, b: a + b, neg)
            term = -nsum if term is None else term - nsum
        if mag != 1.0:
            term = term * mag
        total = term if total is None else total + term
    return total


def _dft_slab_kernel(a_ref, tc_ref, ts_ref, re_ref, im_ref):
    cos, sin = _dft_tables(DFT_SLABS)
    for lb in range(a_ref.shape[3] // V7X_LANES):
        sl = slice(lb * V7X_LANES, (lb + 1) * V7X_LANES)
        xs = [a_ref[0, n2, :, sl].astype(F32) for n2 in range(DFT_SLABS)]
        for k2 in range(DFT_SLABS):
            u = _lincomb(cos[k2], xs)
            v = _lincomb(sin[k2], xs)
            tc = tc_ref[k2]
            ts = ts_ref[k2]
            if v is None:
                re, im = u * tc, -(u * ts)
            else:
                re, im = u * tc - v * ts, -(v * tc) - u * ts
            re_ref[0, k2, :, sl] = re.astype(re_ref.dtype)
            im_ref[0, k2, :, sl] = im.astype(im_ref.dtype)


def _dft_dense_kernel(re_ref, im_ref, cs_ref, snc_ref, p_ref, q_ref):
    t = jnp.concatenate([re_ref[0, 0], im_ref[0, 0]], axis=0)
    p_ref[0, 0] = jnp.dot(cs_ref[...], t, preferred_element_type=F32).astype(p_ref.dtype)
    q_ref[0, 0] = jnp.dot(snc_ref[...], t, preferred_element_type=F32).astype(q_ref.dtype)


def _chan_dft_kernel(p_ref, q_ref, csn_ref, o_ref):
    pq = jnp.concatenate([p_ref[...], q_ref[...]], axis=1)
    o_ref[...] = jnp.dot(pq, csn_ref[...], preferred_element_type=F32).astype(o_ref.dtype)


def _fourier_real_2d(a, bsz, length):
    m, d = a.shape
    l2 = DFT_SLABS
    l1 = length // l2
    cg = d // FOURIER_GROUPS
    c1, s1 = _dft_tables(l1)
    cs1 = jnp.asarray(np.concatenate([c1, s1], axis=1), BF16)
    snc1 = jnp.asarray(np.concatenate([s1, -c1], axis=1), BF16)
    k2 = np.arange(l2, dtype=np.int64)[:, None]
    n1 = np.arange(l1, dtype=np.int64)[None, :]
    tw = 2.0 * np.pi * ((n1 * k2) % length) / length
    lanes = np.ones((1, 1, V7X_LANES))
    tc = jnp.asarray(np.cos(tw)[:, :, None] * lanes, F32)
    ts = jnp.asarray(np.sin(tw)[:, :, None] * lanes, F32)
    norm = 1.0 / math.sqrt(length * cg)
    cc, sc = _dft_tables(cg)
    csn = jnp.asarray(np.concatenate([cc, -sc], axis=0) * norm, BF16)

    rt = _largest_block(l1, 32, 16)
    ct = _largest_block(d, 1024, V7X_LANES)
    slab_spec = pl.BlockSpec((1, l2, rt, ct), lambda b, i, j: (b, 0, i, j))
    tw_spec = pl.BlockSpec((l2, rt, V7X_LANES), lambda b, i, j: (0, i, 0))
    slab_shape = jax.ShapeDtypeStruct((bsz, l2, l1, d), BF16)
    t_re, t_im = pl.pallas_call(
        _dft_slab_kernel,
        grid=(bsz, l1 // rt, d // ct),
        in_specs=[slab_spec, tw_spec, tw_spec],
        out_specs=[slab_spec, slab_spec],
        out_shape=[slab_shape, slab_shape],
        compiler_params=_params(("arbitrary",) * 3, _vmem_limit(3 * _nbytes((l2, rt, ct), BF16),
                                                                 2 * _nbytes((l2, rt, V7X_LANES), F32))),
        name="dft_pos_slabs",
    )(a.reshape(bsz, l2, l1, d), tc, ts)

    ct2 = _largest_block(d, 512, V7X_LANES)
    mat1 = pl.BlockSpec((l1, 2 * l1), lambda b, s, j: (0, 0))
    blk = pl.BlockSpec((1, 1, l1, ct2), lambda b, s, j: (b, s, 0, j))
    p, q = pl.pallas_call(
        _dft_dense_kernel,
        grid=(bsz, l2, d // ct2),
        in_specs=[blk, blk, mat1, mat1],
        out_specs=[blk, blk],
        out_shape=[slab_shape, slab_shape],
        compiler_params=_params(("arbitrary",) * 3, _vmem_limit(4 * _nbytes((l1, ct2), BF16),
                                                                 2 * _nbytes((l1, 2 * l1), BF16),
                                                                 scratch=4 * _nbytes((l1, ct2), F32))),
        name="dft_pos_dense",
    )(t_re, t_im, cs1, snc1)

    bm = _largest_block(m, 1024, 16)
    pq_spec = pl.BlockSpec((bm, cg), lambda i, g: (i, g))
    mat_c = pl.BlockSpec((2 * cg, cg), lambda i, g: (0, 0))
    y = pl.pallas_call(
        _chan_dft_kernel,
        grid=(m // bm, FOURIER_GROUPS),
        in_specs=[pq_spec, pq_spec, mat_c],
        out_specs=pq_spec,
        out_shape=jax.ShapeDtypeStruct((m, d), BF16),
        compiler_params=_params(("arbitrary",) * 2, _vmem_limit(3 * _nbytes((bm, cg), BF16),
                                                                 2 * _nbytes((cg, cg), BF16),
                                                                 scratch=2 * _nbytes((bm, cg), F32))),
        name="dft_channel",
    )(p.reshape(m, d), q.reshape(m, d), csn)
    return y.reshape(bsz, l2, l1, d).transpose(0, 2, 1, 3).reshape(m, d)


def _conv_aligned_span(width):
    first = CONV_HALO - (width - 1) // 2
    return (first + width - 1) // F32_SUBLANES * F32_SUBLANES


def _dwconv_kernel(prev_ref, cur_ref, next_ref, w_ref, b_ref, o_ref, win_sc, sh_sc, *, width, rows_chunk):
    i = pl.program_id(1)
    bt = cur_ref.shape[1]
    first = CONV_HALO - (width - 1) // 2
    win_sc[0:CONV_HALO] = jnp.where(i > 0, prev_ref[0], 0.0)
    win_sc[CONV_HALO:CONV_HALO + bt] = cur_ref[0]
    win_sc[CONV_HALO + bt:] = jnp.where(i < pl.num_programs(1) - 1, next_ref[0], 0.0)
    span = bt + _conv_aligned_span(width)
    for r in range(1, F32_SUBLANES):
        sh_sc[r - 1] = win_sc[r:r + span]
    for c in range(bt // rows_chunk):
        acc = None
        for k in range(width):
            r = (first + k) % F32_SUBLANES
            a = first + k - r + c * rows_chunk
            rows = win_sc[a:a + rows_chunk] if r == 0 else sh_sc[r - 1, a:a + rows_chunk]
            term = rows * w_ref[k:k + 1]
            acc = term if acc is None else acc + term
        o_ref[0, c * rows_chunk:(c + 1) * rows_chunk] = acc + b_ref[...]


def _depthwise_conv(u, w_dw, b_dw, bsz, length):
    m, c = u.shape
    width = w_dw.shape[0]
    assert (width - 1) // 2 <= CONV_HALO
    assert _conv_aligned_span(width) + F32_SUBLANES <= 2 * CONV_HALO
    bt = _largest_block(length, 128, CONV_HALO)
    cw = _largest_block(c, 512, V7X_LANES)
    rows_chunk = _largest_block(bt, 32, F32_SUBLANES)
    per_block = bt // CONV_HALO
    n_halo = length // CONV_HALO
    u3 = u.reshape(bsz, length, c)
    halo = lambda f: pl.BlockSpec((1, CONV_HALO, cw), f)
    out = pl.pallas_call(
        functools.partial(_dwconv_kernel, width=width, rows_chunk=rows_chunk),
        grid=(bsz, length // bt, c // cw),
        in_specs=[halo(lambda b, i, j: (b, jnp.maximum(i * per_block - 1, 0), j)),
                  pl.BlockSpec((1, bt, cw), lambda b, i, j: (b, i, j)),
                  halo(lambda b, i, j: (b, jnp.minimum((i + 1) * per_block, n_halo - 1), j)),
                  pl.BlockSpec((width, cw), lambda b, i, j: (0, j)),
                  pl.BlockSpec((1, cw), lambda b, i, j: (0, j))],
        out_specs=pl.BlockSpec((1, bt, cw), lambda b, i, j: (b, i, j)),
        out_shape=jax.ShapeDtypeStruct((bsz, length, c), F32),
        scratch_shapes=[pltpu.VMEM((bt + 2 * CONV_HALO, cw), F32),
                        pltpu.VMEM((F32_SUBLANES - 1, bt + _conv_aligned_span(width), cw), F32)],
        compiler_params=_params(("arbitrary",) * 3,
                                _vmem_limit(4 * _nbytes((bt, cw), F32),
                                            scratch=F32_SUBLANES * _nbytes((bt + 2 * CONV_HALO, cw), F32))),
        name="depthwise_conv",
    )(u3, u3, u3, w_dw, b_dw.reshape(1, c))
    return out.reshape(m, c)


def kernel(x, c, ctx, c_ctx, ada_down, ada_up, ada_b, norm_w, ffn_w_in, ffn_w_out, fourier_w, fourier_b,
           attn_w_qkv, attn_q_norm, attn_k_norm, attn_w_o, conv_w_pw1, conv_b_pw1, conv_w_dw, conv_b_dw,
           conv_ln_w, conv_ln_b, conv_w_pw2, conv_b_pw2, final_norm_w):
    bsz, seq, d = x.shape
    ctx_len = ctx.shape[1]
    depth = ada_down.shape[0]
    assert bsz + 1 <= MOD_ROWS and d % (FOURIER_GROUPS * V7X_LANES) == 0 and seq % GRID_W == 0
    assert seq % (16 * DFT_SLABS) == 0 and ctx_len % (16 * DFT_SLABS) == 0

    cond = jnp.zeros((MOD_ROWS, d), F32).at[:bsz].set(c).at[bsz].set(c_ctx)
    mod = _ada_modulation(cond, ada_down, ada_up, ada_b)

    lat = _Groups(0, seq)
    con = _Groups(bsz, None)
    def norm_of(layer, s, groups):
        if layer >= depth:
            return None
        return _Norm(norm_w[layer, s], mod[layer][:, 3 * s], mod[layer][:, 3 * s + 1], groups)

    def ctx_in(layer):
        return layer < depth and (layer < depth - 1 or layer % N_MIXERS == 1)

    s_lat = _prep_stream(x.reshape(bsz * seq, d), norm_of(0, 0, lat))
    s_ctx = _prep_stream(ctx.reshape(bsz * ctx_len, d), norm_of(0, 0, con))
    rope = _rope_tables(seq)
    ffn_w_in, ffn_w_out, fourier_w, attn_w_qkv, attn_w_o, conv_w_pw1, conv_w_pw2 = (
        w.astype(BF16) for w in (ffn_w_in, ffn_w_out, fourier_w, attn_w_qkv, attn_w_o, conv_w_pw1, conv_w_pw2))

    for i in range(depth):
        kind = i % N_MIXERS
        j = i // N_MIXERS
        ctx_out = i < depth - 1
        m = mod[i]

        def ffn(s, which, nxt_lat, nxt_ctx, run_ctx):
            nonlocal s_lat, s_ctx
            w_in, w_out = _Weight(ffn_w_in, (i, which)), _Weight(ffn_w_out, (i, which))
            c = _shift_mm(norm_of(i, s, lat), w_in, None)
            gate = m[:, 3 * s + 2]
            act = _gated_mm(s_lat, lat, w_in, c, "swiglu", BF16)
            s_lat = _res_mm(act, w_out, None, s_lat, gate, FFN_RES_WEIGHT, lat, nxt_lat)
            if run_ctx:
                act = _gated_mm(s_ctx, con, w_in, c, "swiglu", BF16)
                s_ctx = _res_mm(act, w_out, None, s_ctx, gate, FFN_RES_WEIGHT, con, nxt_ctx)

        ffn(0, 0, norm_of(i, 1, lat), norm_of(i, 1, con), ctx_in(i))

        n_lat, n_ctx = norm_of(i, 1, lat), norm_of(i, 1, con)
        nx_lat, nx_ctx = norm_of(i, 2, lat), norm_of(i, 2, con)
        gate = m[:, 5]
        if kind == 0:
            w, b = _Weight(fourier_w, (j,)), fourier_b[j]
            y = _fourier_real_2d(_norm_apply(s_lat, n_lat), bsz, seq)
            s_lat = _res_mm(y, w, b, s_lat, gate, 1.0, lat, nx_lat)
            if ctx_out:
                y = _fourier_real_2d(_norm_apply(s_ctx, n_ctx), bsz, ctx_len)
                s_ctx = _res_mm(y, w, b, s_ctx, gate, 1.0, con, nx_ctx)
        elif kind == 1:
            w_qkv, w_o = _Weight(attn_w_qkv, (j,)), _Weight(attn_w_o, (j,))
            kvw = (w_qkv.shape[1] - d) // 2
            c = _shift_mm(n_lat, w_qkv, None)
            qkv_x = _qkv_project(s_lat, lat, w_qkv, c, attn_q_norm[j], attn_k_norm[j], rope, seq)
            qkv_c = _qkv_project(s_ctx, con, w_qkv, c, attn_q_norm[j], attn_k_norm[j], None, ctx_len)
            qkv_x = qkv_x.reshape(bsz, seq, -1)
            qkv_c = qkv_c.reshape(bsz, ctx_len, -1)
            o_x = _attention(qkv_x, [qkv_x, qkv_c], d, kvw).reshape(bsz * seq, d)
            s_lat = _res_mm(o_x, w_o, None, s_lat, gate, 1.0, lat, nx_lat)
            if ctx_out:
                o_c = _attention(qkv_c, [qkv_c], d, kvw).reshape(bsz * ctx_len, d)
                s_ctx = _res_mm(o_c, w_o, None, s_ctx, gate, 1.0, con, nx_ctx)
        else:
            w1, w2 = _Weight(conv_w_pw1, (j,)), _Weight(conv_w_pw2, (j,))
            c = _shift_mm(n_lat, w1, conv_b_pw1[j])

            def conv(stream, groups, length):
                u = _gated_mm(stream, groups, w1, c, "glu", F32)
                u = _depthwise_conv(u, conv_w_dw[j], conv_b_dw[j], bsz, length)
                return _ln_silu(u, conv_ln_w[j], conv_ln_b[j])

            s_lat = _res_mm(conv(s_lat, lat, seq), w2, conv_b_pw2[j], s_lat, gate, 1.0, lat, nx_lat)
            if ctx_out:
                s_ctx = _res_mm(conv(s_ctx, con, ctx_len), w2, conv_b_pw2[j], s_ctx, gate, 1.0, con, nx_ctx)

        ffn(2, 1, norm_of(i + 1, 0, lat), norm_of(i + 1, 0, con) if ctx_in(i + 1) else None, ctx_out)

    return _rms_norm(s_lat.x, final_norm_w).reshape(bsz, seq, d)
```

```python
import functools
import math
from typing import NamedTuple

import jax
import jax.numpy as jnp
import numpy as np
from jax import lax
from jax.experimental import pallas as pl
from jax.experimental.pallas import tpu as pltpu

F32 = jnp.float32
BF16 = jnp.bfloat16

HEAD_DIM = 128
GQA_GROUP = 4
GRID_W = 64
FOURIER_GROUPS = 4
N_MIXERS = 3
N_MOD = 9
FFN_RES_WEIGHT = 0.5
ROPE_THETA = 10000.0
EPS = 1e-6

V7X_VMEM_BYTES = 64 * 1024 * 1024
V7X_LANES = 128
F32_SUBLANES = 8
MOD_ROWS = 8
CONV_HALO = 16


def _vmem_limit(*block_bytes, scratch=0):
    need = 2 * sum(block_bytes) + scratch
    return int(min(V7X_VMEM_BYTES - (6 << 20), max(2 * need, 16 << 20)))


def _params(sem, vmem):
    return pltpu.CompilerParams(dimension_semantics=sem, vmem_limit_bytes=vmem)


def _largest_block(n, cap, mult):
    if n <= cap:
        return n
    b = (cap // mult) * mult
    while b >= mult:
        if n % b == 0:
            return b
        b -= mult
    raise ValueError(f"no block for {n} (cap {cap}, multiple {mult})")


def _nbytes(shape, dtype):
    return int(np.prod(shape)) * jnp.dtype(dtype).itemsize


def _silu(x):
    return x * jax.nn.sigmoid(x)


def _ada_down_kernel(c_ref, w_ref, o_ref):
    s = _silu(c_ref[...])
    o_ref[0] = jnp.dot(s.astype(BF16), w_ref[0].astype(BF16), preferred_element_type=F32)


def _ada_up_kernel(h_ref, w_ref, b_ref, o_ref):
    o_ref[0] = jnp.dot(h_ref[0].astype(BF16), w_ref[0].astype(BF16),
                       preferred_element_type=F32) + b_ref[0]


def _ada_modulation(cond, down, up, bias):
    depth, d, rank = down.shape
    nout = up.shape[2]
    bn = _largest_block(rank, 256, V7X_LANES)
    h = pl.pallas_call(
        _ada_down_kernel,
        grid=(depth, rank // bn),
        in_specs=[pl.BlockSpec((MOD_ROWS, d), lambda l, j: (0, 0)),
                  pl.BlockSpec((1, d, bn), lambda l, j: (l, 0, j))],
        out_specs=pl.BlockSpec((1, MOD_ROWS, bn), lambda l, j: (l, 0, j)),
        out_shape=jax.ShapeDtypeStruct((depth, MOD_ROWS, rank), F32),
        compiler_params=_params(("arbitrary", "arbitrary"),
                                _vmem_limit(_nbytes((d, bn), F32), _nbytes((MOD_ROWS, d), F32))),
        name="ada_down",
    )(cond, down)
    bn = _largest_block(nout, 2048, V7X_LANES)
    m = pl.pallas_call(
        _ada_up_kernel,
        grid=(depth, nout // bn),
        in_specs=[pl.BlockSpec((1, MOD_ROWS, rank), lambda l, j: (l, 0, 0)),
                  pl.BlockSpec((1, rank, bn), lambda l, j: (l, 0, j)),
                  pl.BlockSpec((1, 1, bn), lambda l, j: (l, 0, j))],
        out_specs=pl.BlockSpec((1, MOD_ROWS, bn), lambda l, j: (l, 0, j)),
        out_shape=jax.ShapeDtypeStruct((depth, MOD_ROWS, nout), F32),
        compiler_params=_params(("arbitrary", "arbitrary"), _vmem_limit(_nbytes((rank, bn), F32))),
        name="ada_up",
    )(h, up, bias.reshape(depth, 1, nout))
    return m.reshape(depth, MOD_ROWS, N_MOD, d)


class _Groups:
    def __init__(self, first, rows_per_group):
        self.first = first
        self.rows_per_group = rows_per_group

    def index(self, i, bm):
        if self.rows_per_group is None:
            return self.first
        return self.first + (i * bm) // self.rows_per_group

    def batch_index(self, b):
        return self.first if self.rows_per_group is None else self.first + b

    def block_rows(self, m, cap):
        limit = m if self.rows_per_group is None else self.rows_per_group
        return _largest_block(limit, cap, F32_SUBLANES * 2)


class _Norm(NamedTuple):
    w: jax.Array
    shift: jax.Array
    scale: jax.Array
    groups: _Groups

    def gain_specs_and_args(self, bm, bn, col):
        r, d = self.scale.shape
        g = self.groups
        return ([pl.BlockSpec((1, bn), lambda *idx: (0, col(*idx))),
                 pl.BlockSpec((1, 1, bn), lambda *idx: (g.index(idx[0], bm), 0, col(*idx)))],
                [self.w.reshape(1, d), self.scale.reshape(r, 1, d)])


class _Stream(NamedTuple):
    x: jax.Array
    xg: jax.Array
    sq: jax.Array


def _lane_partial_sq(x):
    sq = x * x
    part = sq[:, :V7X_LANES]
    for t in range(1, x.shape[1] // V7X_LANES):
        part = part + sq[:, t * V7X_LANES:(t + 1) * V7X_LANES]
    return part


def _row_rstd(sq_ref, d):
    ms = jnp.sum(sq_ref[...], axis=-1, keepdims=True) / d
    return jnp.broadcast_to(lax.rsqrt(ms + EPS), sq_ref.shape)


def _scale_rows(acc, rstd):
    return jnp.concatenate([acc[:, t * V7X_LANES:(t + 1) * V7X_LANES] * rstd
                            for t in range(acc.shape[1] // V7X_LANES)], axis=1)


def _prep_kernel(x_ref, nw_ref, sc_ref, xg_ref, sq_ref):
    x = x_ref[...]
    xg_ref[...] = (x * (nw_ref[...] * (1.0 + sc_ref[0]))).astype(xg_ref.dtype)
    sq_ref[...] = _lane_partial_sq(x)


def _prep_stream(x, norm):
    m, d = x.shape
    bm = norm.groups.block_rows(m, 256)
    row = lambda n: pl.BlockSpec((bm, n), lambda i: (i, 0))
    gspecs, gargs = norm.gain_specs_and_args(bm, d, lambda i: 0)
    xg, sq = pl.pallas_call(
        _prep_kernel,
        grid=(m // bm,),
        in_specs=[row(d)] + gspecs,
        out_specs=[row(d), row(V7X_LANES)],
        out_shape=[jax.ShapeDtypeStruct((m, d), BF16), jax.ShapeDtypeStruct((m, V7X_LANES), F32)],
        compiler_params=_params(("arbitrary",), _vmem_limit(_nbytes((bm, d), F32), _nbytes((bm, d), BF16))),
        name="prep_stream",
    )(x, *gargs)
    return _Stream(x, xg, sq)


def _rms_norm_kernel(x_ref, nw_ref, o_ref):
    x = x_ref[...]
    o_ref[...] = x * lax.rsqrt(jnp.mean(x * x, axis=-1, keepdims=True) + EPS) * nw_ref[...]


def _rms_norm(x, norm_w):
    m, d = x.shape
    bm = _largest_block(m, 256, F32_SUBLANES)
    return pl.pallas_call(
        _rms_norm_kernel,
        grid=(m // bm,),
        in_specs=[pl.BlockSpec((bm, d), lambda i: (i, 0)), pl.BlockSpec((1, d), lambda i: (0, 0))],
        out_specs=pl.BlockSpec((bm, d), lambda i: (i, 0)),
        out_shape=jax.ShapeDtypeStruct((m, d), F32),
        compiler_params=_params(("arbitrary",), _vmem_limit(2 * _nbytes((bm, d), F32))),
        name="final_rms_norm",
    )(x, norm_w.reshape(1, d))


def _ln_silu_kernel(u_ref, w_ref, b_ref, o_ref):
    u = u_ref[...]
    xc = u - jnp.mean(u, axis=-1, keepdims=True)
    y = xc * lax.rsqrt(jnp.mean(xc * xc, axis=-1, keepdims=True) + EPS)
    o_ref[...] = _silu(y * w_ref[...] + b_ref[...]).astype(o_ref.dtype)


def _ln_silu(u, w, b):
    m, d = u.shape
    bm = _largest_block(m, 256, F32_SUBLANES * 2)
    vec = pl.BlockSpec((1, d), lambda i: (0, 0))
    return pl.pallas_call(
        _ln_silu_kernel,
        grid=(m // bm,),
        in_specs=[pl.BlockSpec((bm, d), lambda i: (i, 0)), vec, vec],
        out_specs=pl.BlockSpec((bm, d), lambda i: (i, 0)),
        out_shape=jax.ShapeDtypeStruct((m, d), BF16),
        compiler_params=_params(("arbitrary",), _vmem_limit(_nbytes((bm, d), F32), _nbytes((bm, d), BF16))),
        name="ln_silu",
    )(u, w.reshape(1, d), b.reshape(1, d))


class _Weight(NamedTuple):
    array: jax.Array
    lead: tuple

    @property
    def shape(self):
        return self.array.shape[len(self.lead):]

    def spec(self, bn, col):
        lead = self.lead
        k = self.shape[0]
        return pl.BlockSpec((None,) * len(lead) + (k, bn), lambda *idx: (*lead, 0, col(*idx)))


def _shift_mm_kernel(*refs, has_bias):
    s_ref, w_ref = refs[:2]
    o_ref = refs[-1]
    y = jnp.dot(s_ref[...].astype(BF16), w_ref[...], preferred_element_type=F32)
    o_ref[...] = y + refs[2][...] if has_bias else y


def _shift_mm(norm, w, bias):
    r, k = norm.shift.shape
    n = w.shape[1]
    bn = _largest_block(n, 1024, V7X_LANES)
    in_specs = [pl.BlockSpec((r, k), lambda j: (0, 0)), w.spec(bn, lambda j: j)]
    args = [norm.shift, w.array]
    if bias is not None:
        in_specs.append(pl.BlockSpec((1, bn), lambda j: (0, j)))
        args.append(bias.reshape(1, n))
    out = pl.pallas_call(
        functools.partial(_shift_mm_kernel, has_bias=bias is not None),
        grid=(n // bn,),
        in_specs=in_specs,
        out_specs=pl.BlockSpec((r, bn), lambda j: (0, j)),
        out_shape=jax.ShapeDtypeStruct((r, n), F32),
        compiler_params=_params(("arbitrary",), _vmem_limit(_nbytes((k, bn), BF16), _nbytes((r, k), F32))),
        name="shift_mm",
    )(*args)
    return out.reshape(r, 1, n)


def _gated_mm_kernel(xg_ref, sq_ref, wa_ref, wb_ref, ca_ref, cb_ref, o_ref, rstd_sc, *, mode):
    @pl.when(pl.program_id(1) == 0)
    def _():
        rstd_sc[...] = _row_rstd(sq_ref, xg_ref.shape[1])

    a = xg_ref[...]
    rstd = rstd_sc[...]
    ga = _scale_rows(jnp.dot(a, wa_ref[...], preferred_element_type=F32), rstd) + ca_ref[0]
    gb = _scale_rows(jnp.dot(a, wb_ref[...], preferred_element_type=F32), rstd) + cb_ref[0]
    r = _silu(ga) * gb if mode == "swiglu" else ga * jax.nn.sigmoid(gb)
    o_ref[...] = r.astype(o_ref.dtype)


def _gated_mm(stream, groups, w, c, mode, out_dtype):
    m, k = stream.xg.shape
    f = w.shape[1] // 2
    bm = groups.block_rows(m, 1024)
    bn = _largest_block(f, 512, V7X_LANES)
    nb = f // bn
    g = groups
    add = lambda off: pl.BlockSpec((1, 1, bn), lambda i, j: (g.index(i, bm), 0, j + off))
    return pl.pallas_call(
        functools.partial(_gated_mm_kernel, mode=mode),
        grid=(m // bm, nb),
        in_specs=[pl.BlockSpec((bm, k), lambda i, j: (i, 0)),
                  pl.BlockSpec((bm, V7X_LANES), lambda i, j: (i, 0)),
                  w.spec(bn, lambda i, j: j), w.spec(bn, lambda i, j: j + nb), add(0), add(nb)],
        out_specs=pl.BlockSpec((bm, bn), lambda i, j: (i, j)),
        out_shape=jax.ShapeDtypeStruct((m, f), out_dtype),
        scratch_shapes=[pltpu.VMEM((bm, V7X_LANES), F32)],
        compiler_params=_params(("arbitrary", "arbitrary"),
                                _vmem_limit(_nbytes((bm, k), BF16), 2 * _nbytes((k, bn), BF16),
                                            _nbytes((bm, bn), out_dtype), scratch=4 * _nbytes((bm, bn), F32))),
        name=f"gated_mm_{mode}",
    )(stream.xg, stream.sq, w.array, w.array, c, c)


def _res_mm_kernel(*refs, coef, has_bias, emit_stream):
    a_ref, w_ref = refs[:2]
    n_in = 5 if has_bias else 4
    x_ref, g_ref = refs[n_in - 2:n_in]
    y = jnp.dot(a_ref[...], w_ref[...], preferred_element_type=F32)
    if has_bias:
        y = y + refs[2][...]
    g = g_ref[0] if coef == 1.0 else coef * g_ref[0]
    xn = x_ref[...] + g * y
    if not emit_stream:
        refs[n_in][...] = xn
        return
    nw_ref, sc_ref, o_ref, xg_ref, sq_ref = refs[n_in:]
    o_ref[...] = xn
    xg_ref[...] = (xn * (nw_ref[...] * (1.0 + sc_ref[0]))).astype(xg_ref.dtype)
    part = _lane_partial_sq(xn)
    j = pl.program_id(1)

    @pl.when(j == 0)
    def _():
        sq_ref[...] = part

    @pl.when(j > 0)
    def _():
        sq_ref[...] += part


def _res_mm(a, w, bias, stream, gate, coef, groups, next_norm):
    x = stream.x
    m, k = a.shape
    n = w.shape[1]
    bm = groups.block_rows(m, 1024)
    bn = _largest_block(n, 512, V7X_LANES)
    r = gate.shape[0]
    in_specs = [pl.BlockSpec((bm, k), lambda i, j: (i, 0)), w.spec(bn, lambda i, j: j)]
    args = [a, w.array]
    if bias is not None:
        in_specs.append(pl.BlockSpec((1, bn), lambda i, j: (0, j)))
        args.append(bias.reshape(1, n))
    tile = pl.BlockSpec((bm, bn), lambda i, j: (i, j))
    in_specs += [tile, pl.BlockSpec((1, 1, bn), lambda i, j: (groups.index(i, bm), 0, j))]
    args += [x, gate.reshape(r, 1, n)]
    out_specs = [tile]
    out_shape = [jax.ShapeDtypeStruct((m, n), F32)]
    if next_norm is not None:
        gspecs, gargs = next_norm.gain_specs_and_args(bm, bn, lambda i, j: j)
        in_specs += gspecs
        args += gargs
        out_specs += [tile, pl.BlockSpec((bm, V7X_LANES), lambda i, j: (i, 0))]
        out_shape += [jax.ShapeDtypeStruct((m, n), BF16), jax.ShapeDtypeStruct((m, V7X_LANES), F32)]
    out = pl.pallas_call(
        functools.partial(_res_mm_kernel, coef=coef, has_bias=bias is not None, emit_stream=next_norm is not None),
        grid=(m // bm, n // bn),
        in_specs=in_specs,
        out_specs=out_specs,
        out_shape=out_shape,
        compiler_params=_params(("arbitrary", "arbitrary"),
                                _vmem_limit(_nbytes((bm, k), BF16), _nbytes((k, bn), BF16),
                                            3 * _nbytes((bm, bn), F32), scratch=_nbytes((bm, bn), F32))),
        name="res_mm",
    )(*args)
    return _Stream(*out) if next_norm is not None else _Stream(out[0], None, None)


def _rope_tables(seq_len):
    n_rows = seq_len // GRID_W
    pairs = HEAD_DIM // 4
    row = jnp.broadcast_to(jnp.arange(n_rows, dtype=F32)[:, None], (n_rows, GRID_W)).reshape(-1)
    col = jnp.broadcast_to(jnp.arange(GRID_W, dtype=F32)[None, :], (n_rows, GRID_W)).reshape(-1)
    freqs = ROPE_THETA ** (-jnp.arange(pairs, dtype=F32) / pairs)
    ang = jnp.stack([row[:, None] * freqs, col[:, None] * freqs], axis=1)
    cos = jnp.cos(ang)
    sin = jnp.sin(ang)
    cos_t = jnp.stack([cos, cos], axis=2).reshape(seq_len, HEAD_DIM)
    sin_t = jnp.stack([-sin, sin], axis=2).reshape(seq_len, HEAD_DIM)
    return cos_t, sin_t


def _qkv_kernel(*refs, n_norm_blocks, rope):
    xg_ref, sq_ref, w_ref, c_ref, nw_ref = refs[:5]
    if rope:
        cos_ref, sin_ref = refs[5:7]
    o_ref, rstd_sc = refs[-2:]
    j = pl.program_id(1)

    @pl.when(j == 0)
    def _():
        rstd_sc[...] = _row_rstd(sq_ref, xg_ref.shape[1])

    acc = _scale_rows(jnp.dot(xg_ref[...], w_ref[...], preferred_element_type=F32), rstd_sc[...]) + c_ref[0]
    bn = acc.shape[1]
    pairs = HEAD_DIM // 4

    @pl.when(j < n_norm_blocks)
    def _():
        if rope:
            cos = cos_ref[...]
            sin = sin_ref[...]
            lane = lax.broadcasted_iota(jnp.int32, (1, HEAD_DIM), 1)
            first_half = (lane % (2 * pairs)) < pairs
        for h in range(bn // HEAD_DIM):
            sl = slice(h * HEAD_DIM, (h + 1) * HEAD_DIM)
            t = acc[:, sl]
            y = t * lax.rsqrt(jnp.mean(t * t, axis=-1, keepdims=True) + EPS) * nw_ref[:, sl]
            if rope:
                partner = jnp.where(first_half, pltpu.roll(y, HEAD_DIM - pairs, 1), pltpu.roll(y, pairs, 1))
                y = y * cos + partner * sin
            o_ref[:, sl] = y.astype(o_ref.dtype)

    @pl.when(j >= n_norm_blocks)
    def _():
        o_ref[...] = acc.astype(o_ref.dtype)


def _qkv_project(stream, groups, w_qkv, c, q_norm, k_norm, rope_tables, seq_len):
    m, d = stream.xg.shape
    n = w_qkv.shape[1]
    kv_width = (n - d) // 2
    bm = _largest_block(seq_len, 1024, 16)
    bn = _largest_block(math.gcd(d, kv_width), 512, HEAD_DIM)
    q_scale = HEAD_DIM ** -0.5 * math.log2(math.e)
    nw = jnp.concatenate([jnp.tile(q_norm * q_scale, d // HEAD_DIM), jnp.tile(k_norm, kv_width // HEAD_DIM),
                          jnp.ones((kv_width,), F32)]).reshape(1, n)
    g = groups
    in_specs = [pl.BlockSpec((bm, d), lambda i, j: (i, 0)),
                pl.BlockSpec((bm, V7X_LANES), lambda i, j: (i, 0)),
                w_qkv.spec(bn, lambda i, j: j),
                pl.BlockSpec((1, 1, bn), lambda i, j: (g.index(i, bm), 0, j)),
                pl.BlockSpec((1, bn), lambda i, j: (0, j))]
    args = [stream.xg, stream.sq, w_qkv.array, c, nw]
    if rope_tables is not None:
        pos_blocks = seq_len // bm
        pos_spec = pl.BlockSpec((bm, HEAD_DIM), lambda i, j: (i % pos_blocks, 0))
        in_specs += [pos_spec, pos_spec]
        args += list(rope_tables)
    return pl.pallas_call(
        functools.partial(_qkv_kernel, n_norm_blocks=(d + kv_width) // bn, rope=rope_tables is not None),
        grid=(m // bm, n // bn),
        in_specs=in_specs,
        out_specs=pl.BlockSpec((bm, bn), lambda i, j: (i, j)),
        out_shape=jax.ShapeDtypeStruct((m, n), BF16),
        scratch_shapes=[pltpu.VMEM((bm, V7X_LANES), F32)],
        compiler_params=_params(("arbitrary", "arbitrary"),
                                _vmem_limit(_nbytes((bm, d), BF16), _nbytes((d, bn), BF16), _nbytes((bm, bn), BF16),
                                            scratch=3 * _nbytes((bm, bn), F32))),
        name="qkv_project",
    )(*args)


ATTN_KEY_CHUNK = 1024
ATTN_Q_ROWS = 128
ATTN_MAX_PARTS = 4


def _key_chunks(lengths):
    chunks, col = [], 0
    for src, length in enumerate(lengths):
        size = _largest_block(length, ATTN_KEY_CHUNK, V7X_LANES)
        for start in range(0, length, size):
            chunks.append((src, start, size, col))
            col += size
    return chunks


def _attn_kernel(*refs, n_src):
    q_ref, qn_ref = refs[:2]
    k_refs = refs[2:2 + n_src]
    v_refs = refs[2 + n_src:2 + 2 * n_src]
    o_ref, s_sc, m_sc = refs[2 + 2 * n_src:]
    bq = ATTN_Q_ROWS
    rows = GQA_GROUP * bq
    n_parts = q_ref.shape[1] // bq
    chunks = _key_chunks([k.shape[1] for k in k_refs])

    def q_rows(ref, part):
        return jnp.concatenate([ref[0, part * bq:(part + 1) * bq, g * HEAD_DIM:(g + 1) * HEAD_DIM]
                                for g in range(GQA_GROUP)], axis=0)

    def step(cur, m8_cur, q):
        nxt = 0 if cur is None else cur + 1
        m8 = None
        if cur is not None:
            l8 = jnp.zeros((F32_SUBLANES, rows), F32)
            acc = jnp.zeros((HEAD_DIM, rows), F32)
        for src, start, size, col in chunks:
            st = lax.dot_general(k_refs[src][0, start:start + size, :], q, (((1,), (1,)), ((), ())),
                                 preferred_element_type=F32)
            s_sc[nxt % 2, col:col + size, :] = st
            cm = jnp.max(st.reshape(size // F32_SUBLANES, F32_SUBLANES, rows), axis=0)
            m8 = cm if m8 is None else jnp.maximum(m8, cm)
            if cur is not None:
                s = s_sc[cur % 2, col:col + size, :].reshape(size // F32_SUBLANES, F32_SUBLANES, rows)
                p = jnp.exp2(s - m8_cur[None])
                l8 = l8 + jnp.sum(p, axis=0)
                pt = p.reshape(size, rows).astype(BF16)
                acc = acc + lax.dot_general(v_refs[src][0, start:start + size, :], pt, (((0,), (0,)), ((), ())),
                                            preferred_element_type=F32)
        if cur is not None:
            out_t = acc / jnp.sum(l8, axis=0, keepdims=True)
            for g in range(GQA_GROUP):
                o_ref[0, cur * bq:(cur + 1) * bq, g * HEAD_DIM:(g + 1) * HEAD_DIM] = (
                    out_t[:, g * bq:(g + 1) * bq].T.astype(o_ref.dtype))
        return jnp.broadcast_to(jnp.max(m8, axis=0, keepdims=True), (F32_SUBLANES, rows))

    @pl.when(pl.program_id(2) == 0)
    def _():
        m_sc[...] = step(None, None, q_rows(q_ref, 0))

    m8 = m_sc[...]
    for part in range(n_parts):
        q_next = q_rows(q_ref, part + 1) if part + 1 < n_parts else q_rows(qn_ref, 0)
        m8 = step(part, m8, q_next)
    m_sc[...] = m8


def _attention(q_src, kv_srcs, d, kv_width):
    b, lq, _ = q_src.shape
    n_kv = kv_width // HEAD_DIM
    gw = GQA_GROUP * HEAD_DIM
    assert lq % ATTN_Q_ROWS == 0
    bq = _largest_block(lq, ATTN_Q_ROWS * ATTN_MAX_PARTS, ATTN_Q_ROWS)
    rows = GQA_GROUP * ATTN_Q_ROWS
    k0 = d // HEAD_DIM
    v0 = (d + kv_width) // HEAD_DIM
    n_keys = sum(a.shape[1] for a in kv_srcs)
    n_parts = bq // ATTN_Q_ROWS
    assert n_parts % 2 == 0
    last_part = lq // ATTN_Q_ROWS - 1
    q_spec = pl.BlockSpec((1, bq, gw), lambda bi, hi, qi: (bi, qi, hi))
    qn_spec = pl.BlockSpec((1, ATTN_Q_ROWS, gw),
                           lambda bi, hi, qi: (bi, jnp.minimum((qi + 1) * n_parts, last_part), hi))
    k_specs = [pl.BlockSpec((1, a.shape[1], HEAD_DIM), lambda bi, hi, qi: (bi, 0, k0 + hi)) for a in kv_srcs]
    v_specs = [pl.BlockSpec((1, a.shape[1], HEAD_DIM), lambda bi, hi, qi: (bi, 0, v0 + hi)) for a in kv_srcs]
    return pl.pallas_call(
        functools.partial(_attn_kernel, n_src=len(kv_srcs)),
        grid=(b, n_kv, lq // bq),
        in_specs=[q_spec, qn_spec] + k_specs + v_specs,
        out_specs=pl.BlockSpec((1, bq, gw), lambda bi, hi, qi: (bi, qi, hi)),
        out_shape=jax.ShapeDtypeStruct((b, lq, d), BF16),
        scratch_shapes=[pltpu.VMEM((2, n_keys, rows), F32), pltpu.VMEM((F32_SUBLANES, rows), F32)],
        compiler_params=_params(("arbitrary", "arbitrary", "arbitrary"),
                                _vmem_limit(2 * _nbytes((bq, gw), BF16), 2 * _nbytes((n_keys, HEAD_DIM), BF16),
                                            scratch=2 * _nbytes((rows, n_keys), F32)
                                            + 2 * _nbytes((rows, ATTN_KEY_CHUNK), F32))),
        name="attention",
    )(q_src, q_src, *kv_srcs, *kv_srcs)


DFT_SLABS = 8


def _dft_tables(n):
    idx = np.arange(n, dtype=np.int64)
    ang = 2.0 * np.pi * ((idx[:, None] * idx[None, :]) % n) / n
    return np.cos(ang), np.sin(ang)


def _lincomb(coefs, xs):
    groups = {}
    for cf, x in zip(coefs, xs):
        mag = round(abs(float(cf)), 12)
        if mag == 0.0:
            continue
        pos, neg = groups.setdefault(mag, ([], []))
        (pos if cf > 0 else neg).append(x)
    total = None
    for mag, (pos, neg) in groups.items():
        term = functools.reduce(lambda a, b: a + b, pos) if pos else None
        if neg:
            nsum = functools.reduce(lambda a, b: a + b, neg)
            term = -nsum if term is None else term - nsum
        if mag != 1.0:
            term = term * mag
        total = term if total is None else total + term
    return total


def _dft_slab_kernel(xg_ref, sq_ref, sh_ref, tc_ref, ts_ref, re_ref, im_ref, *, d_model):
    cos, sin = _dft_tables(DFT_SLABS)
    rstd = []
    for n2 in range(DFT_SLABS):
        r = lax.rsqrt(jnp.sum(sq_ref[0, n2], axis=-1, keepdims=True) / d_model + EPS)
        rstd.append(jnp.broadcast_to(r, sq_ref.shape[2:]))
    for lb in range(xg_ref.shape[3] // V7X_LANES):
        sl = slice(lb * V7X_LANES, (lb + 1) * V7X_LANES)
        sh = sh_ref[0, :, sl]
        xs = [xg_ref[0, n2, :, sl].astype(F32) * rstd[n2] + sh for n2 in range(DFT_SLABS)]
        uv = {}
        for k2 in range(DFT_SLABS):
            tc = tc_ref[k2]
            ts = ts_ref[k2]
            mirror = DFT_SLABS - k2
            if mirror in uv:
                u, v = uv[mirror]
                re, im = u * tc + v * ts, v * tc - u * ts
            else:
                u, v = _lincomb(cos[k2], xs), _lincomb(sin[k2], xs)
                uv[k2] = (u, v)
                if v is None:
                    re, im = u * tc, -(u * ts)
                else:
                    re, im = u * tc - v * ts, -(v * tc) - u * ts
            re_ref[0, k2, :, sl] = re.astype(re_ref.dtype)
            im_ref[0, k2, :, sl] = im.astype(im_ref.dtype)


def _dft_dense_kernel(re_ref, im_ref, cs_ref, snc_ref, p_ref, q_ref):
    t = jnp.concatenate([re_ref[0, 0], im_ref[0, 0]], axis=0)
    p_ref[0, 0] = jnp.dot(cs_ref[...], t, preferred_element_type=F32).astype(p_ref.dtype)
    q_ref[0, 0] = jnp.dot(snc_ref[...], t, preferred_element_type=F32).astype(q_ref.dtype)


def _chan_dft_kernel(p_ref, q_ref, csn_ref, o_ref):
    pq = jnp.concatenate([p_ref[...], q_ref[...]], axis=1)
    o_ref[...] = jnp.dot(pq, csn_ref[...], preferred_element_type=F32).astype(o_ref.dtype)


def _fourier_real_2d(stream, norm, bsz, length):
    m, d = stream.xg.shape
    l2 = DFT_SLABS
    l1 = length // l2
    cg = d // FOURIER_GROUPS
    c1, s1 = _dft_tables(l1)
    cs1 = jnp.asarray(np.concatenate([c1, s1], axis=1), BF16)
    snc1 = jnp.asarray(np.concatenate([s1, -c1], axis=1), BF16)
    k2 = np.arange(l2, dtype=np.int64)[:, None]
    n1 = np.arange(l1, dtype=np.int64)[None, :]
    tw = 2.0 * np.pi * ((n1 * k2) % length) / length
    lanes = np.ones((1, 1, V7X_LANES))
    tc = jnp.asarray(np.cos(tw)[:, :, None] * lanes, F32)
    ts = jnp.asarray(np.sin(tw)[:, :, None] * lanes, F32)
    ortho = 1.0 / math.sqrt(length * cg)
    cc, sc = _dft_tables(cg)
    csn = jnp.asarray(np.concatenate([cc, -sc], axis=0) * ortho, BF16)

    rt = _largest_block(l1, 32, 16)
    ct = _largest_block(d, 1024, V7X_LANES)
    slab_spec = pl.BlockSpec((1, l2, rt, ct), lambda b, i, j: (b, 0, i, j))
    sq_spec = pl.BlockSpec((1, l2, rt, V7X_LANES), lambda b, i, j: (b, 0, i, 0))
    tw_spec = pl.BlockSpec((l2, rt, V7X_LANES), lambda b, i, j: (0, i, 0))
    r = norm.shift.shape[0]
    g = norm.groups
    sh_spec = pl.BlockSpec((1, 1, ct), lambda b, i, j: (g.batch_index(b), 0, j))
    slab_shape = jax.ShapeDtypeStruct((bsz, l2, l1, d), BF16)
    t_re, t_im = pl.pallas_call(
        functools.partial(_dft_slab_kernel, d_model=d),
        grid=(bsz, l1 // rt, d // ct),
        in_specs=[slab_spec, sq_spec, sh_spec, tw_spec, tw_spec],
        out_specs=[slab_spec, slab_spec],
        out_shape=[slab_shape, slab_shape],
        compiler_params=_params(("arbitrary",) * 3, _vmem_limit(3 * _nbytes((l2, rt, ct), BF16),
                                                                 3 * _nbytes((l2, rt, V7X_LANES), F32))),
        name="dft_pos_slabs",
    )(stream.xg.reshape(bsz, l2, l1, d), stream.sq.reshape(bsz, l2, l1, V7X_LANES),
      norm.shift.reshape(r, 1, d), tc, ts)

    ct2 = _largest_block(d, 512, V7X_LANES)
    mat1 = pl.BlockSpec((l1, 2 * l1), lambda b, s, j: (0, 0))
    blk = pl.BlockSpec((1, 1, l1, ct2), lambda b, s, j: (b, s, 0, j))
    p, q = pl.pallas_call(
        _dft_dense_kernel,
        grid=(bsz, l2, d // ct2),
        in_specs=[blk, blk, mat1, mat1],
        out_specs=[blk, blk],
        out_shape=[slab_shape, slab_shape],
        compiler_params=_params(("arbitrary",) * 3, _vmem_limit(4 * _nbytes((l1, ct2), BF16),
                                                                 2 * _nbytes((l1, 2 * l1), BF16),
                                                                 scratch=4 * _nbytes((l1, ct2), F32))),
        name="dft_pos_dense",
    )(t_re, t_im, cs1, snc1)

    bm = _largest_block(m, 1024, 16)
    pq_spec = pl.BlockSpec((bm, cg), lambda i, g: (i, g))
    mat_c = pl.BlockSpec((2 * cg, cg), lambda i, g: (0, 0))
    y = pl.pallas_call(
        _chan_dft_kernel,
        grid=(m // bm, FOURIER_GROUPS),
        in_specs=[pq_spec, pq_spec, mat_c],
        out_specs=pq_spec,
        out_shape=jax.ShapeDtypeStruct((m, d), BF16),
        compiler_params=_params(("arbitrary",) * 2, _vmem_limit(3 * _nbytes((bm, cg), BF16),
                                                                 2 * _nbytes((cg, cg), BF16),
                                                                 scratch=2 * _nbytes((bm, cg), F32))),
        name="dft_channel",
    )(p.reshape(m, d), q.reshape(m, d), csn)
    return y.reshape(bsz, l2, l1, d).transpose(0, 2, 1, 3).reshape(m, d)


def _conv_aligned_span(width):
    first = CONV_HALO - (width - 1) // 2
    return (first + width - 1) // F32_SUBLANES * F32_SUBLANES


def _dwconv_kernel(prev_ref, cur_ref, next_ref, w_ref, b_ref, o_ref, win_sc, sh_sc, *, width, rows_chunk):
    i = pl.program_id(1)
    bt = cur_ref.shape[1]
    first = CONV_HALO - (width - 1) // 2
    win_sc[0:CONV_HALO] = jnp.where(i > 0, prev_ref[0], 0.0)
    win_sc[CONV_HALO:CONV_HALO + bt] = cur_ref[0]
    win_sc[CONV_HALO + bt:] = jnp.where(i < pl.num_programs(1) - 1, next_ref[0], 0.0)
    span = bt + _conv_aligned_span(width)
    for r in range(1, F32_SUBLANES):
        sh_sc[r - 1] = win_sc[r:r + span]
    for c in range(bt // rows_chunk):
        acc = None
        for k in range(width):
            r = (first + k) % F32_SUBLANES
            a = first + k - r + c * rows_chunk
            rows = win_sc[a:a + rows_chunk] if r == 0 else sh_sc[r - 1, a:a + rows_chunk]
            term = rows * w_ref[k:k + 1]
            acc = term if acc is None else acc + term
        o_ref[0, c * rows_chunk:(c + 1) * rows_chunk] = acc + b_ref[...]


def _depthwise_conv(u, w_dw, b_dw, bsz, length):
    m, c = u.shape
    width = w_dw.shape[0]
    assert (width - 1) // 2 <= CONV_HALO
    assert _conv_aligned_span(width) + F32_SUBLANES <= 2 * CONV_HALO
    bt = _largest_block(length, 128, CONV_HALO)
    cw = _largest_block(c, 512, V7X_LANES)
    rows_chunk = _largest_block(bt, 32, F32_SUBLANES)
    per_block = bt // CONV_HALO
    n_halo = length // CONV_HALO
    u3 = u.reshape(bsz, length, c)
    halo = lambda f: pl.BlockSpec((1, CONV_HALO, cw), f)
    out = pl.pallas_call(
        functools.partial(_dwconv_kernel, width=width, rows_chunk=rows_chunk),
        grid=(bsz, length // bt, c // cw),
        in_specs=[halo(lambda b, i, j: (b, jnp.maximum(i * per_block - 1, 0), j)),
                  pl.BlockSpec((1, bt, cw), lambda b, i, j: (b, i, j)),
                  halo(lambda b, i, j: (b, jnp.minimum((i + 1) * per_block, n_halo - 1), j)),
                  pl.BlockSpec((width, cw), lambda b, i, j: (0, j)),
                  pl.BlockSpec((1, cw), lambda b, i, j: (0, j))],
        out_specs=pl.BlockSpec((1, bt, cw), lambda b, i, j: (b, i, j)),
        out_shape=jax.ShapeDtypeStruct((bsz, length, c), F32),
        scratch_shapes=[pltpu.VMEM((bt + 2 * CONV_HALO, cw), F32),
                        pltpu.VMEM((F32_SUBLANES - 1, bt + _conv_aligned_span(width), cw), F32)],
        compiler_params=_params(("arbitrary",) * 3,
                                _vmem_limit(4 * _nbytes((bt, cw), F32),
                                            scratch=F32_SUBLANES * _nbytes((bt + 2 * CONV_HALO, cw), F32))),
        name="depthwise_conv",
    )(u3, u3, u3, w_dw, b_dw.reshape(1, c))
    return out.reshape(m, c)


def kernel(x, c, ctx, c_ctx, ada_down, ada_up, ada_b, norm_w, ffn_w_in, ffn_w_out, fourier_w, fourier_b,
           attn_w_qkv, attn_q_norm, attn_k_norm, attn_w_o, conv_w_pw1, conv_b_pw1, conv_w_dw, conv_b_dw,
           conv_ln_w, conv_ln_b, conv_w_pw2, conv_b_pw2, final_norm_w):
    bsz, seq, d = x.shape
    ctx_len = ctx.shape[1]
    depth = ada_down.shape[0]
    assert bsz + 1 <= MOD_ROWS and d % (FOURIER_GROUPS * V7X_LANES) == 0 and seq % GRID_W == 0
    assert seq % (16 * DFT_SLABS) == 0 and ctx_len % (16 * DFT_SLABS) == 0

    cond = jnp.zeros((MOD_ROWS, d), F32).at[:bsz].set(c).at[bsz].set(c_ctx)
    mod = _ada_modulation(cond, ada_down, ada_up, ada_b)

    lat = _Groups(0, seq)
    con = _Groups(bsz, None)
    def norm_of(layer, s, groups):
        if layer >= depth:
            return None
        return _Norm(norm_w[layer, s], mod[layer][:, 3 * s], mod[layer][:, 3 * s + 1], groups)

    def ctx_in(layer):
        return layer < depth and (layer < depth - 1 or layer % N_MIXERS == 1)

    s_lat = _prep_stream(x.reshape(bsz * seq, d), norm_of(0, 0, lat))
    s_ctx = _prep_stream(ctx.reshape(bsz * ctx_len, d), norm_of(0, 0, con))
    rope = _rope_tables(seq)
    ffn_w_in, ffn_w_out, fourier_w, attn_w_qkv, attn_w_o, conv_w_pw1, conv_w_pw2 = (
        w.astype(BF16) for w in (ffn_w_in, ffn_w_out, fourier_w, attn_w_qkv, attn_w_o, conv_w_pw1, conv_w_pw2))

    for i in range(depth):
        kind = i % N_MIXERS
        j = i // N_MIXERS
        ctx_out = i < depth - 1
        m = mod[i]

        def ffn(s, which, nxt_lat, nxt_ctx, run_ctx):
            nonlocal s_lat, s_ctx
            w_in, w_out = _Weight(ffn_w_in, (i, which)), _Weight(ffn_w_out, (i, which))
            c = _shift_mm(norm_of(i, s, lat), w_in, None)
            gate = m[:, 3 * s + 2]
            act = _gated_mm(s_lat, lat, w_in, c, "swiglu", BF16)
            s_lat = _res_mm(act, w_out, None, s_lat, gate, FFN_RES_WEIGHT, lat, nxt_lat)
            if run_ctx:
                act = _gated_mm(s_ctx, con, w_in, c, "swiglu", BF16)
                s_ctx = _res_mm(act, w_out, None, s_ctx, gate, FFN_RES_WEIGHT, con, nxt_ctx)

        ffn(0, 0, norm_of(i, 1, lat), norm_of(i, 1, con), ctx_in(i))

        n_lat, n_ctx = norm_of(i, 1, lat), norm_of(i, 1, con)
        nx_lat, nx_ctx = norm_of(i, 2, lat), norm_of(i, 2, con)
        gate = m[:, 5]
        if kind == 0:
            w, b = _Weight(fourier_w, (j,)), fourier_b[j]
            y = _fourier_real_2d(s_lat, n_lat, bsz, seq)
            s_lat = _res_mm(y, w, b, s_lat, gate, 1.0, lat, nx_lat)
            if ctx_out:
                y = _fourier_real_2d(s_ctx, n_ctx, bsz, ctx_len)
                s_ctx = _res_mm(y, w, b, s_ctx, gate, 1.0, con, nx_ctx)
        elif kind == 1:
            w_qkv, w_o = _Weight(attn_w_qkv, (j,)), _Weight(attn_w_o, (j,))
            kvw = (w_qkv.shape[1] - d) // 2
            c = _shift_mm(n_lat, w_qkv, None)
            qkv_x = _qkv_project(s_lat, lat, w_qkv, c, attn_q_norm[j], attn_k_norm[j], rope, seq)
            qkv_c = _qkv_project(s_ctx, con, w_qkv, c, attn_q_norm[j], attn_k_norm[j], None, ctx_len)
            qkv_x = qkv_x.reshape(bsz, seq, -1)
            qkv_c = qkv_c.reshape(bsz, ctx_len, -1)
            o_x = _attention(qkv_x, [qkv_x, qkv_c], d, kvw).reshape(bsz * seq, d)
            s_lat = _res_mm(o_x, w_o, None, s_lat, gate, 1.0, lat, nx_lat)
            if ctx_out:
                o_c = _attention(qkv_c, [qkv_c], d, kvw).reshape(bsz * ctx_len, d)
                s_ctx = _res_mm(o_c, w_o, None, s_ctx, gate, 1.0, con, nx_ctx)
        else:
            w1, w2 = _Weight(conv_w_pw1, (j,)), _Weight(conv_w_pw2, (j,))
            c = _shift_mm(n_lat, w1, conv_b_pw1[j])

            def conv(stream, groups, length):
                u = _gated_mm(stream, groups, w1, c, "glu", F32)
                u = _depthwise_conv(u, conv_w_dw[j], conv_b_dw[j], bsz, length)
                return _ln_silu(u, conv_ln_w[j], conv_ln_b[j])

            s_lat = _res_mm(conv(s_lat, lat, seq), w2, conv_b_pw2[j], s_lat, gate, 1.0, lat, nx_lat)
            if ctx_out:
                s_ctx = _res_mm(conv(s_ctx, con, ctx_len), w2, conv_b_pw2[j], s_ctx, gate, 1.0, con, nx_ctx)

        ffn(2, 1, norm_of(i + 1, 0, lat), norm_of(i + 1, 0, con) if ctx_in(i + 1) else None, ctx_out)

    return _rms_norm(s_lat.x, final_norm_w).reshape(bsz, seq, d)
```

```python
import functools
import math
from typing import NamedTuple

import jax
import jax.numpy as jnp
import numpy as np
from jax import lax
from jax.experimental import pallas as pl
from jax.experimental.pallas import tpu as pltpu

F32 = jnp.float32
BF16 = jnp.bfloat16

HEAD_DIM = 128
GQA_GROUP = 4
GRID_W = 64
FOURIER_GROUPS = 4
N_MIXERS = 3
N_MOD = 9
FFN_RES_WEIGHT = 0.5
ROPE_THETA = 10000.0
EPS = 1e-6

V7X_VMEM_BYTES = 64 * 1024 * 1024
V7X_VMEM_REQUEST_CAP = V7X_VMEM_BYTES - 6 * 1024 * 1024
V7X_VMEM_REQUEST_MIN = 16 * 1024 * 1024
V7X_LANES = 128
F32_SUBLANES = 8
MOD_ROWS = 8
CONV_HALO = 16

MM_ROWS = 1024
MM_COLS = 512
ROW_TILE = 256


def _vmem_limit(*block_bytes, scratch=0):
    need = 2 * sum(block_bytes) + scratch
    return int(min(V7X_VMEM_REQUEST_CAP, max(2 * need, V7X_VMEM_REQUEST_MIN)))


def _params(sem, vmem):
    return pltpu.CompilerParams(dimension_semantics=sem, vmem_limit_bytes=vmem)


def _largest_block(n, cap, mult):
    if n <= cap:
        return n
    b = (cap // mult) * mult
    while b >= mult:
        if n % b == 0:
            return b
        b -= mult
    raise ValueError(f"no block for {n} (cap {cap}, multiple {mult})")


def _nbytes(shape, dtype):
    return int(np.prod(shape)) * jnp.dtype(dtype).itemsize


def _silu(x):
    return x * jax.nn.sigmoid(x)


def _ada_down_kernel(c_ref, w_ref, o_ref):
    s = _silu(c_ref[...])
    o_ref[0] = jnp.dot(s.astype(BF16), w_ref[0].astype(BF16), preferred_element_type=F32)


def _ada_up_kernel(h_ref, w_ref, b_ref, o_ref):
    o_ref[0] = jnp.dot(h_ref[0].astype(BF16), w_ref[0].astype(BF16),
                       preferred_element_type=F32) + b_ref[0]


def _ada_modulation(cond, down, up, bias):
    depth, d, rank = down.shape
    nout = up.shape[2]
    bn = _largest_block(rank, 256, V7X_LANES)
    h = pl.pallas_call(
        _ada_down_kernel,
        grid=(depth, rank // bn),
        in_specs=[pl.BlockSpec((MOD_ROWS, d), lambda l, j: (0, 0)),
                  pl.BlockSpec((1, d, bn), lambda l, j: (l, 0, j))],
        out_specs=pl.BlockSpec((1, MOD_ROWS, bn), lambda l, j: (l, 0, j)),
        out_shape=jax.ShapeDtypeStruct((depth, MOD_ROWS, rank), F32),
        compiler_params=_params(("arbitrary", "arbitrary"),
                                _vmem_limit(_nbytes((d, bn), F32), _nbytes((MOD_ROWS, d), F32))),
        name="ada_down",
    )(cond, down)
    bn = _largest_block(nout, 2048, V7X_LANES)
    m = pl.pallas_call(
        _ada_up_kernel,
        grid=(depth, nout // bn),
        in_specs=[pl.BlockSpec((1, MOD_ROWS, rank), lambda l, j: (l, 0, 0)),
                  pl.BlockSpec((1, rank, bn), lambda l, j: (l, 0, j)),
                  pl.BlockSpec((1, 1, bn), lambda l, j: (l, 0, j))],
        out_specs=pl.BlockSpec((1, MOD_ROWS, bn), lambda l, j: (l, 0, j)),
        out_shape=jax.ShapeDtypeStruct((depth, MOD_ROWS, nout), F32),
        compiler_params=_params(("arbitrary", "arbitrary"), _vmem_limit(_nbytes((rank, bn), F32))),
        name="ada_up",
    )(h, up, bias.reshape(depth, 1, nout))
    return m.reshape(depth, MOD_ROWS, N_MOD, d)


class _Groups:
    def __init__(self, first, rows_per_group):
        self.first = first
        self.rows_per_group = rows_per_group

    def index(self, i, bm):
        if self.rows_per_group is None:
            return self.first
        return self.first + (i * bm) // self.rows_per_group

    def batch_index(self, b):
        return self.first if self.rows_per_group is None else self.first + b

    def block_rows(self, m, cap):
        limit = m if self.rows_per_group is None else self.rows_per_group
        return _largest_block(limit, cap, F32_SUBLANES * 2)


class _Norm(NamedTuple):
    w: jax.Array
    shift: jax.Array
    scale: jax.Array
    groups: _Groups

    def gain_specs_and_args(self, bm, bn, col):
        r, d = self.scale.shape
        g = self.groups
        return ([pl.BlockSpec((1, bn), lambda *idx: (0, col(*idx))),
                 pl.BlockSpec((1, 1, bn), lambda *idx: (g.index(idx[0], bm), 0, col(*idx)))],
                [self.w.reshape(1, d), self.scale.reshape(r, 1, d)])


class _Stream(NamedTuple):
    x: jax.Array
    xg: jax.Array
    sq: jax.Array


def _lane_partial_sq(x):
    sq = x * x
    part = sq[:, :V7X_LANES]
    for t in range(1, x.shape[1] // V7X_LANES):
        part = part + sq[:, t * V7X_LANES:(t + 1) * V7X_LANES]
    return part


def _row_rstd(sq_ref, d):
    ms = jnp.sum(sq_ref[...], axis=-1, keepdims=True) / d
    return jnp.broadcast_to(lax.rsqrt(ms + EPS), sq_ref.shape)


def _scale_rows(acc, rstd):
    return jnp.concatenate([acc[:, t * V7X_LANES:(t + 1) * V7X_LANES] * rstd
                            for t in range(acc.shape[1] // V7X_LANES)], axis=1)


def _prep_kernel(x_ref, nw_ref, sc_ref, xg_ref, sq_ref):
    x = x_ref[...]
    xg_ref[...] = (x * (nw_ref[...] * (1.0 + sc_ref[0]))).astype(xg_ref.dtype)
    sq_ref[...] = _lane_partial_sq(x)


def _prep_stream(x, norm):
    m, d = x.shape
    bm = norm.groups.block_rows(m, ROW_TILE)
    row = lambda n: pl.BlockSpec((bm, n), lambda i: (i, 0))
    gspecs, gargs = norm.gain_specs_and_args(bm, d, lambda i: 0)
    xg, sq = pl.pallas_call(
        _prep_kernel,
        grid=(m // bm,),
        in_specs=[row(d)] + gspecs,
        out_specs=[row(d), row(V7X_LANES)],
        out_shape=[jax.ShapeDtypeStruct((m, d), BF16), jax.ShapeDtypeStruct((m, V7X_LANES), F32)],
        compiler_params=_params(("arbitrary",), _vmem_limit(_nbytes((bm, d), F32), _nbytes((bm, d), BF16))),
        name="prep_stream",
    )(x, *gargs)
    return _Stream(x, xg, sq)


def _rms_norm_kernel(x_ref, nw_ref, o_ref):
    x = x_ref[...]
    o_ref[...] = x * lax.rsqrt(jnp.mean(x * x, axis=-1, keepdims=True) + EPS) * nw_ref[...]


def _rms_norm(x, norm_w):
    m, d = x.shape
    bm = _largest_block(m, ROW_TILE, F32_SUBLANES)
    return pl.pallas_call(
        _rms_norm_kernel,
        grid=(m // bm,),
        in_specs=[pl.BlockSpec((bm, d), lambda i: (i, 0)), pl.BlockSpec((1, d), lambda i: (0, 0))],
        out_specs=pl.BlockSpec((bm, d), lambda i: (i, 0)),
        out_shape=jax.ShapeDtypeStruct((m, d), F32),
        compiler_params=_params(("arbitrary",), _vmem_limit(2 * _nbytes((bm, d), F32))),
        name="final_rms_norm",
    )(x, norm_w.reshape(1, d))


def _ln_silu_kernel(u_ref, w_ref, b_ref, o_ref):
    u = u_ref[...]
    xc = u - jnp.mean(u, axis=-1, keepdims=True)
    y = xc * lax.rsqrt(jnp.mean(xc * xc, axis=-1, keepdims=True) + EPS)
    o_ref[...] = _silu(y * w_ref[...] + b_ref[...]).astype(o_ref.dtype)


def _ln_silu(u, w, b):
    m, d = u.shape
    bm = _largest_block(m, ROW_TILE, F32_SUBLANES * 2)
    vec = pl.BlockSpec((1, d), lambda i: (0, 0))
    return pl.pallas_call(
        _ln_silu_kernel,
        grid=(m // bm,),
        in_specs=[pl.BlockSpec((bm, d), lambda i: (i, 0)), vec, vec],
        out_specs=pl.BlockSpec((bm, d), lambda i: (i, 0)),
        out_shape=jax.ShapeDtypeStruct((m, d), BF16),
        compiler_params=_params(("arbitrary",), _vmem_limit(_nbytes((bm, d), F32), _nbytes((bm, d), BF16))),
        name="ln_silu",
    )(u, w.reshape(1, d), b.reshape(1, d))


class _Weight(NamedTuple):
    array: jax.Array
    lead: tuple

    @property
    def shape(self):
        return self.array.shape[len(self.lead):]

    def spec(self, bn, col):
        lead = self.lead
        k = self.shape[0]
        return pl.BlockSpec((None,) * len(lead) + (k, bn), lambda *idx: (*lead, 0, col(*idx)))


def _shift_mm_kernel(*refs, has_bias):
    s_ref, w_ref = refs[:2]
    o_ref = refs[-1]
    y = jnp.dot(s_ref[...].astype(BF16), w_ref[...], preferred_element_type=F32)
    o_ref[...] = y + refs[2][...] if has_bias else y


def _shift_mm(norm, w, bias):
    r, k = norm.shift.shape
    n = w.shape[1]
    bn = _largest_block(n, 1024, V7X_LANES)
    in_specs = [pl.BlockSpec((r, k), lambda j: (0, 0)), w.spec(bn, lambda j: j)]
    args = [norm.shift, w.array]
    if bias is not None:
        in_specs.append(pl.BlockSpec((1, bn), lambda j: (0, j)))
        args.append(bias.reshape(1, n))
    out = pl.pallas_call(
        functools.partial(_shift_mm_kernel, has_bias=bias is not None),
        grid=(n // bn,),
        in_specs=in_specs,
        out_specs=pl.BlockSpec((r, bn), lambda j: (0, j)),
        out_shape=jax.ShapeDtypeStruct((r, n), F32),
        compiler_params=_params(("arbitrary",), _vmem_limit(_nbytes((k, bn), BF16), _nbytes((r, k), F32))),
        name="shift_mm",
    )(*args)
    return out.reshape(r, 1, n)


def _gated_mm_kernel(xg_ref, sq_ref, wa_ref, wb_ref, ca_ref, cb_ref, o_ref, rstd_sc, *, mode):
    @pl.when(pl.program_id(1) == 0)
    def _():
        rstd_sc[...] = _row_rstd(sq_ref, xg_ref.shape[1])

    a = xg_ref[...]
    rstd = rstd_sc[...]
    ga = _scale_rows(jnp.dot(a, wa_ref[...], preferred_element_type=F32), rstd) + ca_ref[0]
    gb = _scale_rows(jnp.dot(a, wb_ref[...], preferred_element_type=F32), rstd) + cb_ref[0]
    r = _silu(ga) * gb if mode == "swiglu" else ga * jax.nn.sigmoid(gb)
    o_ref[...] = r.astype(o_ref.dtype)


def _gated_mm(stream, groups, w, c, mode, out_dtype):
    m, k = stream.xg.shape
    f = w.shape[1] // 2
    bm = groups.block_rows(m, MM_ROWS)
    bn = _largest_block(f, MM_COLS, V7X_LANES)
    nb = f // bn
    g = groups
    add = lambda off: pl.BlockSpec((1, 1, bn), lambda i, j: (g.index(i, bm), 0, j + off))
    return pl.pallas_call(
        functools.partial(_gated_mm_kernel, mode=mode),
        grid=(m // bm, nb),
        in_specs=[pl.BlockSpec((bm, k), lambda i, j: (i, 0)),
                  pl.BlockSpec((bm, V7X_LANES), lambda i, j: (i, 0)),
                  w.spec(bn, lambda i, j: j), w.spec(bn, lambda i, j: j + nb), add(0), add(nb)],
        out_specs=pl.BlockSpec((bm, bn), lambda i, j: (i, j)),
        out_shape=jax.ShapeDtypeStruct((m, f), out_dtype),
        scratch_shapes=[pltpu.VMEM((bm, V7X_LANES), F32)],
        compiler_params=_params(("arbitrary", "arbitrary"),
                                _vmem_limit(_nbytes((bm, k), BF16), 2 * _nbytes((k, bn), BF16),
                                            _nbytes((bm, bn), out_dtype), scratch=4 * _nbytes((bm, bn), F32))),
        name=f"gated_mm_{mode}",
    )(stream.xg, stream.sq, w.array, w.array, c, c)


def _res_mm_kernel(*refs, coef, has_bias, emit_stream):
    a_ref, w_ref = refs[:2]
    n_in = 5 if has_bias else 4
    x_ref, g_ref = refs[n_in - 2:n_in]
    y = jnp.dot(a_ref[...], w_ref[...], preferred_element_type=F32)
    if has_bias:
        y = y + refs[2][...]
    g = g_ref[0] if coef == 1.0 else coef * g_ref[0]
    xn = x_ref[...] + g * y
    if not emit_stream:
        refs[n_in][...] = xn
        return
    nw_ref, sc_ref, o_ref, xg_ref, sq_ref = refs[n_in:]
    o_ref[...] = xn
    xg_ref[...] = (xn * (nw_ref[...] * (1.0 + sc_ref[0]))).astype(xg_ref.dtype)
    part = _lane_partial_sq(xn)
    j = pl.program_id(1)

    @pl.when(j == 0)
    def _():
        sq_ref[...] = part

    @pl.when(j > 0)
    def _():
        sq_ref[...] += part


def _res_mm(a, w, bias, stream, gate, coef, groups, next_norm):
    x = stream.x
    m, k = a.shape
    n = w.shape[1]
    bm = groups.block_rows(m, MM_ROWS)
    bn = _largest_block(n, MM_COLS, V7X_LANES)
    r = gate.shape[0]
    in_specs = [pl.BlockSpec((bm, k), lambda i, j: (i, 0)), w.spec(bn, lambda i, j: j)]
    args = [a, w.array]
    if bias is not None:
        in_specs.append(pl.BlockSpec((1, bn), lambda i, j: (0, j)))
        args.append(bias.reshape(1, n))
    tile = pl.BlockSpec((bm, bn), lambda i, j: (i, j))
    in_specs += [tile, pl.BlockSpec((1, 1, bn), lambda i, j: (groups.index(i, bm), 0, j))]
    args += [x, gate.reshape(r, 1, n)]
    out_specs = [tile]
    out_shape = [jax.ShapeDtypeStruct((m, n), F32)]
    if next_norm is not None:
        gspecs, gargs = next_norm.gain_specs_and_args(bm, bn, lambda i, j: j)
        in_specs += gspecs
        args += gargs
        out_specs += [tile, pl.BlockSpec((bm, V7X_LANES), lambda i, j: (i, 0))]
        out_shape += [jax.ShapeDtypeStruct((m, n), BF16), jax.ShapeDtypeStruct((m, V7X_LANES), F32)]
    out = pl.pallas_call(
        functools.partial(_res_mm_kernel, coef=coef, has_bias=bias is not None, emit_stream=next_norm is not None),
        grid=(m // bm, n // bn),
        in_specs=in_specs,
        out_specs=out_specs,
        out_shape=out_shape,
        compiler_params=_params(("arbitrary", "arbitrary"),
                                _vmem_limit(_nbytes((bm, k), BF16), _nbytes((k, bn), BF16),
                                            3 * _nbytes((bm, bn), F32), scratch=_nbytes((bm, bn), F32))),
        name="res_mm",
    )(*args)
    return _Stream(*out) if next_norm is not None else _Stream(out[0], None, None)


def _rope_tables(seq_len):
    n_rows = seq_len // GRID_W
    pairs = HEAD_DIM // 4
    row = jnp.broadcast_to(jnp.arange(n_rows, dtype=F32)[:, None], (n_rows, GRID_W)).reshape(-1)
    col = jnp.broadcast_to(jnp.arange(GRID_W, dtype=F32)[None, :], (n_rows, GRID_W)).reshape(-1)
    freqs = ROPE_THETA ** (-jnp.arange(pairs, dtype=F32) / pairs)
    ang = jnp.stack([row[:, None] * freqs, col[:, None] * freqs], axis=1)
    cos = jnp.cos(ang)
    sin = jnp.sin(ang)
    cos_t = jnp.stack([cos, cos], axis=2).reshape(seq_len, HEAD_DIM)
    sin_t = jnp.stack([-sin, sin], axis=2).reshape(seq_len, HEAD_DIM)
    return cos_t, sin_t


def _qkv_kernel(*refs, n_norm_blocks, rope):
    xg_ref, sq_ref, w_ref, c_ref, nw_ref = refs[:5]
    if rope:
        cos_ref, sin_ref = refs[5:7]
    o_ref, rstd_sc = refs[-2:]
    j = pl.program_id(1)

    @pl.when(j == 0)
    def _():
        rstd_sc[...] = _row_rstd(sq_ref, xg_ref.shape[1])

    acc = _scale_rows(jnp.dot(xg_ref[...], w_ref[...], preferred_element_type=F32), rstd_sc[...]) + c_ref[0]
    bn = acc.shape[1]
    pairs = HEAD_DIM // 4

    @pl.when(j < n_norm_blocks)
    def _():
        if rope:
            cos = cos_ref[...]
            sin = sin_ref[...]
            lane = lax.broadcasted_iota(jnp.int32, (1, HEAD_DIM), 1)
            first_half = (lane % (2 * pairs)) < pairs
        for h in range(bn // HEAD_DIM):
            sl = slice(h * HEAD_DIM, (h + 1) * HEAD_DIM)
            t = acc[:, sl]
            y = t * lax.rsqrt(jnp.mean(t * t, axis=-1, keepdims=True) + EPS) * nw_ref[:, sl]
            if rope:
                partner = jnp.where(first_half, pltpu.roll(y, HEAD_DIM - pairs, 1), pltpu.roll(y, pairs, 1))
                y = y * cos + partner * sin
            o_ref[:, sl] = y.astype(o_ref.dtype)

    @pl.when(j >= n_norm_blocks)
    def _():
        o_ref[...] = acc.astype(o_ref.dtype)


def _qkv_project(stream, groups, w_qkv, c, q_norm, k_norm, rope_tables, seq_len):
    m, d = stream.xg.shape
    n = w_qkv.shape[1]
    kv_width = (n - d) // 2
    bm = _largest_block(seq_len, MM_ROWS, 16)
    bn = _largest_block(math.gcd(d, kv_width), MM_COLS, HEAD_DIM)
    q_scale = HEAD_DIM ** -0.5 * math.log2(math.e)
    nw = jnp.concatenate([jnp.tile(q_norm * q_scale, d // HEAD_DIM), jnp.tile(k_norm, kv_width // HEAD_DIM),
                          jnp.ones((kv_width,), F32)]).reshape(1, n)
    g = groups
    in_specs = [pl.BlockSpec((bm, d), lambda i, j: (i, 0)),
                pl.BlockSpec((bm, V7X_LANES), lambda i, j: (i, 0)),
                w_qkv.spec(bn, lambda i, j: j),
                pl.BlockSpec((1, 1, bn), lambda i, j: (g.index(i, bm), 0, j)),
                pl.BlockSpec((1, bn), lambda i, j: (0, j))]
    args = [stream.xg, stream.sq, w_qkv.array, c, nw]
    if rope_tables is not None:
        pos_blocks = seq_len // bm
        pos_spec = pl.BlockSpec((bm, HEAD_DIM), lambda i, j: (i % pos_blocks, 0))
        in_specs += [pos_spec, pos_spec]
        args += list(rope_tables)
    return pl.pallas_call(
        functools.partial(_qkv_kernel, n_norm_blocks=(d + kv_width) // bn, rope=rope_tables is not None),
        grid=(m // bm, n // bn),
        in_specs=in_specs,
        out_specs=pl.BlockSpec((bm, bn), lambda i, j: (i, j)),
        out_shape=jax.ShapeDtypeStruct((m, n), BF16),
        scratch_shapes=[pltpu.VMEM((bm, V7X_LANES), F32)],
        compiler_params=_params(("arbitrary", "arbitrary"),
                                _vmem_limit(_nbytes((bm, d), BF16), _nbytes((d, bn), BF16), _nbytes((bm, bn), BF16),
                                            scratch=3 * _nbytes((bm, bn), F32))),
        name="qkv_project",
    )(*args)


ATTN_KEY_CHUNK = 1024
ATTN_Q_ROWS = 128
ATTN_MAX_PARTS = 4


def _key_chunks(lengths):
    chunks, col = [], 0
    for src, length in enumerate(lengths):
        size = _largest_block(length, ATTN_KEY_CHUNK, V7X_LANES)
        for start in range(0, length, size):
            chunks.append((src, start, size, col))
            col += size
    return chunks


def _attn_kernel(*refs, n_src):
    q_ref, qn_ref = refs[:2]
    k_refs = refs[2:2 + n_src]
    v_refs = refs[2 + n_src:2 + 2 * n_src]
    o_ref, s_sc, m_sc = refs[2 + 2 * n_src:]
    bq = ATTN_Q_ROWS
    rows = GQA_GROUP * bq
    n_parts = q_ref.shape[1] // bq
    chunks = _key_chunks([k.shape[1] for k in k_refs])

    def q_rows(ref, part):
        return jnp.concatenate([ref[0, part * bq:(part + 1) * bq, g * HEAD_DIM:(g + 1) * HEAD_DIM]
                                for g in range(GQA_GROUP)], axis=0)

    def step(cur, m8_cur, q):
        nxt = 0 if cur is None else cur + 1
        m8 = None
        if cur is not None:
            l8 = jnp.zeros((F32_SUBLANES, rows), F32)
            acc = jnp.zeros((HEAD_DIM, rows), F32)
        for src, start, size, col in chunks:
            st = lax.dot_general(k_refs[src][0, start:start + size, :], q, (((1,), (1,)), ((), ())),
                                 preferred_element_type=F32)
            s_sc[nxt % 2, col:col + size, :] = st
            cm = jnp.max(st.reshape(size // F32_SUBLANES, F32_SUBLANES, rows), axis=0)
            m8 = cm if m8 is None else jnp.maximum(m8, cm)
            if cur is not None:
                s = s_sc[cur % 2, col:col + size, :].reshape(size // F32_SUBLANES, F32_SUBLANES, rows)
                p = jnp.exp2(s - m8_cur[None])
                l8 = l8 + jnp.sum(p, axis=0)
                pt = p.reshape(size, rows).astype(BF16)
                acc = acc + lax.dot_general(v_refs[src][0, start:start + size, :], pt, (((0,), (0,)), ((), ())),
                                            preferred_element_type=F32)
        if cur is not None:
            out_t = acc / jnp.sum(l8, axis=0, keepdims=True)
            for g in range(GQA_GROUP):
                o_ref[0, cur * bq:(cur + 1) * bq, g * HEAD_DIM:(g + 1) * HEAD_DIM] = (
                    out_t[:, g * bq:(g + 1) * bq].T.astype(o_ref.dtype))
        return jnp.broadcast_to(jnp.max(m8, axis=0, keepdims=True), (F32_SUBLANES, rows))

    @pl.when(pl.program_id(2) == 0)
    def _():
        m_sc[...] = step(None, None, q_rows(q_ref, 0))

    m8 = m_sc[...]
    for part in range(n_parts):
        q_next = q_rows(q_ref, part + 1) if part + 1 < n_parts else q_rows(qn_ref, 0)
        m8 = step(part, m8, q_next)
    m_sc[...] = m8


def _attention(q_src, kv_srcs, d, kv_width):
    b, lq, _ = q_src.shape
    n_kv = kv_width // HEAD_DIM
    gw = GQA_GROUP * HEAD_DIM
    assert lq % ATTN_Q_ROWS == 0
    bq = _largest_block(lq, ATTN_Q_ROWS * ATTN_MAX_PARTS, ATTN_Q_ROWS)
    rows = GQA_GROUP * ATTN_Q_ROWS
    k0 = d // HEAD_DIM
    v0 = (d + kv_width) // HEAD_DIM
    n_keys = sum(a.shape[1] for a in kv_srcs)
    n_parts = bq // ATTN_Q_ROWS
    assert n_parts % 2 == 0
    last_part = lq // ATTN_Q_ROWS - 1
    q_spec = pl.BlockSpec((1, bq, gw), lambda bi, hi, qi: (bi, qi, hi))
    qn_spec = pl.BlockSpec((1, ATTN_Q_ROWS, gw),
                           lambda bi, hi, qi: (bi, jnp.minimum((qi + 1) * n_parts, last_part), hi))
    k_specs = [pl.BlockSpec((1, a.shape[1], HEAD_DIM), lambda bi, hi, qi: (bi, 0, k0 + hi)) for a in kv_srcs]
    v_specs = [pl.BlockSpec((1, a.shape[1], HEAD_DIM), lambda bi, hi, qi: (bi, 0, v0 + hi)) for a in kv_srcs]
    return pl.pallas_call(
        functools.partial(_attn_kernel, n_src=len(kv_srcs)),
        grid=(b, n_kv, lq // bq),
        in_specs=[q_spec, qn_spec] + k_specs + v_specs,
        out_specs=pl.BlockSpec((1, bq, gw), lambda bi, hi, qi: (bi, qi, hi)),
        out_shape=jax.ShapeDtypeStruct((b, lq, d), BF16),
        scratch_shapes=[pltpu.VMEM((2, n_keys, rows), F32), pltpu.VMEM((F32_SUBLANES, rows), F32)],
        compiler_params=_params(("arbitrary", "arbitrary", "arbitrary"),
                                _vmem_limit(2 * _nbytes((bq, gw), BF16), 2 * _nbytes((n_keys, HEAD_DIM), BF16),
                                            scratch=2 * _nbytes((rows, n_keys), F32)
                                            + 2 * _nbytes((rows, ATTN_KEY_CHUNK), F32))),
        name="attention",
    )(q_src, q_src, *kv_srcs, *kv_srcs)


DFT_SLABS = 8


def _dft_tables(n):
    idx = np.arange(n, dtype=np.int64)
    ang = 2.0 * np.pi * ((idx[:, None] * idx[None, :]) % n) / n
    return np.cos(ang), np.sin(ang)


def _lincomb(coefs, xs):
    groups = {}
    for cf, x in zip(coefs, xs):
        mag = round(abs(float(cf)), 12)
        if mag == 0.0:
            continue
        pos, neg = groups.setdefault(mag, ([], []))
        (pos if cf > 0 else neg).append(x)
    total = None
    for mag, (pos, neg) in groups.items():
        term = functools.reduce(lambda a, b: a + b, pos) if pos else None
        if neg:
            nsum = functools.reduce(lambda a, b: a + b, neg)
            term = -nsum if term is None else term - nsum
        if mag != 1.0:
            term = term * mag
        total = term if total is None else total + term
    return total


def _dft_slab_kernel(xg_ref, sq_ref, sh_ref, tc_ref, ts_ref, re_ref, im_ref, *, d_model):
    cos, sin = _dft_tables(DFT_SLABS)
    rstd = []
    for n2 in range(DFT_SLABS):
        r = lax.rsqrt(jnp.sum(sq_ref[0, n2], axis=-1, keepdims=True) / d_model + EPS)
        rstd.append(jnp.broadcast_to(r, sq_ref.shape[2:]))
    for lb in range(xg_ref.shape[3] // V7X_LANES):
        sl = slice(lb * V7X_LANES, (lb + 1) * V7X_LANES)
        sh = sh_ref[0, :, sl]
        xs = [xg_ref[0, n2, :, sl].astype(F32) * rstd[n2] + sh for n2 in range(DFT_SLABS)]
        uv = {}
        for k2 in range(DFT_SLABS):
            tc = tc_ref[k2]
            ts = ts_ref[k2]
            mirror = DFT_SLABS - k2
            if mirror in uv:
                u, v = uv[mirror]
                re, im = u * tc + v * ts, v * tc - u * ts
            else:
                u, v = _lincomb(cos[k2], xs), _lincomb(sin[k2], xs)
                uv[k2] = (u, v)
                if v is None:
                    re, im = u * tc, -(u * ts)
                else:
                    re, im = u * tc - v * ts, -(v * tc) - u * ts
            re_ref[0, k2, :, sl] = re.astype(re_ref.dtype)
            im_ref[0, k2, :, sl] = im.astype(im_ref.dtype)


def _dft_dense_kernel(re_ref, im_ref, cs_ref, snc_ref, p_ref, q_ref):
    t = jnp.concatenate([re_ref[0, 0], im_ref[0, 0]], axis=0)
    p_ref[0, 0] = jnp.dot(cs_ref[...], t, preferred_element_type=F32).astype(p_ref.dtype)
    q_ref[0, 0] = jnp.dot(snc_ref[...], t, preferred_element_type=F32).astype(q_ref.dtype)


def _chan_dft_kernel(p_ref, q_ref, csn_ref, o_ref):
    pq = jnp.concatenate([p_ref[...], q_ref[...]], axis=1)
    o_ref[...] = jnp.dot(pq, csn_ref[...], preferred_element_type=F32).astype(o_ref.dtype)


def _fourier_real_2d(stream, norm, bsz, length):
    m, d = stream.xg.shape
    l2 = DFT_SLABS
    l1 = length // l2
    cg = d // FOURIER_GROUPS
    c1, s1 = _dft_tables(l1)
    cs1 = jnp.asarray(np.concatenate([c1, s1], axis=1), BF16)
    snc1 = jnp.asarray(np.concatenate([s1, -c1], axis=1), BF16)
    k2 = np.arange(l2, dtype=np.int64)[:, None]
    n1 = np.arange(l1, dtype=np.int64)[None, :]
    tw = 2.0 * np.pi * ((n1 * k2) % length) / length
    lanes = np.ones((1, 1, V7X_LANES))
    tc = jnp.asarray(np.cos(tw)[:, :, None] * lanes, F32)
    ts = jnp.asarray(np.sin(tw)[:, :, None] * lanes, F32)
    ortho = 1.0 / math.sqrt(length * cg)
    cc, sc = _dft_tables(cg)
    csn = jnp.asarray(np.concatenate([cc, -sc], axis=0) * ortho, BF16)

    rt = _largest_block(l1, 32, 16)
    ct = _largest_block(d, 1024, V7X_LANES)
    slab_spec = pl.BlockSpec((1, l2, rt, ct), lambda b, i, j: (b, 0, i, j))
    sq_spec = pl.BlockSpec((1, l2, rt, V7X_LANES), lambda b, i, j: (b, 0, i, 0))
    tw_spec = pl.BlockSpec((l2, rt, V7X_LANES), lambda b, i, j: (0, i, 0))
    r = norm.shift.shape[0]
    g = norm.groups
    sh_spec = pl.BlockSpec((1, 1, ct), lambda b, i, j: (g.batch_index(b), 0, j))
    slab_shape = jax.ShapeDtypeStruct((bsz, l2, l1, d), BF16)
    t_re, t_im = pl.pallas_call(
        functools.partial(_dft_slab_kernel, d_model=d),
        grid=(bsz, l1 // rt, d // ct),
        in_specs=[slab_spec, sq_spec, sh_spec, tw_spec, tw_spec],
        out_specs=[slab_spec, slab_spec],
        out_shape=[slab_shape, slab_shape],
        compiler_params=_params(("arbitrary",) * 3, _vmem_limit(3 * _nbytes((l2, rt, ct), BF16),
                                                                 3 * _nbytes((l2, rt, V7X_LANES), F32))),
        name="dft_pos_slabs",
    )(stream.xg.reshape(bsz, l2, l1, d), stream.sq.reshape(bsz, l2, l1, V7X_LANES),
      norm.shift.reshape(r, 1, d), tc, ts)

    ct2 = _largest_block(d, MM_COLS, V7X_LANES)
    mat1 = pl.BlockSpec((l1, 2 * l1), lambda b, s, j: (0, 0))
    blk = pl.BlockSpec((1, 1, l1, ct2), lambda b, s, j: (b, s, 0, j))
    p, q = pl.pallas_call(
        _dft_dense_kernel,
        grid=(bsz, l2, d // ct2),
        in_specs=[blk, blk, mat1, mat1],
        out_specs=[blk, blk],
        out_shape=[slab_shape, slab_shape],
        compiler_params=_params(("arbitrary",) * 3, _vmem_limit(4 * _nbytes((l1, ct2), BF16),
                                                                 2 * _nbytes((l1, 2 * l1), BF16),
                                                                 scratch=4 * _nbytes((l1, ct2), F32))),
        name="dft_pos_dense",
    )(t_re, t_im, cs1, snc1)

    bm = _largest_block(m, MM_ROWS, 16)
    pq_spec = pl.BlockSpec((bm, cg), lambda i, g: (i, g))
    mat_c = pl.BlockSpec((2 * cg, cg), lambda i, g: (0, 0))
    y = pl.pallas_call(
        _chan_dft_kernel,
        grid=(m // bm, FOURIER_GROUPS),
        in_specs=[pq_spec, pq_spec, mat_c],
        out_specs=pq_spec,
        out_shape=jax.ShapeDtypeStruct((m, d), BF16),
        compiler_params=_params(("arbitrary",) * 2, _vmem_limit(3 * _nbytes((bm, cg), BF16),
                                                                 2 * _nbytes((cg, cg), BF16),
                                                                 scratch=2 * _nbytes((bm, cg), F32))),
        name="dft_channel",
    )(p.reshape(m, d), q.reshape(m, d), csn)
    return y.reshape(bsz, l2, l1, d).transpose(0, 2, 1, 3).reshape(m, d)


def _conv_aligned_span(width):
    first = CONV_HALO - (width - 1) // 2
    return (first + width - 1) // F32_SUBLANES * F32_SUBLANES


def _dwconv_kernel(prev_ref, cur_ref, next_ref, w_ref, b_ref, o_ref, win_sc, sh_sc, *, width, rows_chunk):
    i = pl.program_id(1)
    bt = cur_ref.shape[1]
    first = CONV_HALO - (width - 1) // 2
    win_sc[0:CONV_HALO] = jnp.where(i > 0, prev_ref[0], 0.0)
    win_sc[CONV_HALO:CONV_HALO + bt] = cur_ref[0]
    win_sc[CONV_HALO + bt:] = jnp.where(i < pl.num_programs(1) - 1, next_ref[0], 0.0)
    span = bt + _conv_aligned_span(width)
    for r in range(1, F32_SUBLANES):
        sh_sc[r - 1] = win_sc[r:r + span]
    for c in range(bt // rows_chunk):
        acc = None
        for k in range(width):
            r = (first + k) % F32_SUBLANES
            a = first + k - r + c * rows_chunk
            rows = win_sc[a:a + rows_chunk] if r == 0 else sh_sc[r - 1, a:a + rows_chunk]
            term = rows * w_ref[k:k + 1]
            acc = term if acc is None else acc + term
        o_ref[0, c * rows_chunk:(c + 1) * rows_chunk] = acc + b_ref[...]


def _depthwise_conv(u, w_dw, b_dw, bsz, length):
    m, c = u.shape
    width = w_dw.shape[0]
    assert (width - 1) // 2 <= CONV_HALO
    assert _conv_aligned_span(width) + F32_SUBLANES <= 2 * CONV_HALO
    bt = _largest_block(length, 128, CONV_HALO)
    cw = _largest_block(c, 512, V7X_LANES)
    rows_chunk = _largest_block(bt, 32, F32_SUBLANES)
    per_block = bt // CONV_HALO
    n_halo = length // CONV_HALO
    u3 = u.reshape(bsz, length, c)
    halo = lambda f: pl.BlockSpec((1, CONV_HALO, cw), f)
    out = pl.pallas_call(
        functools.partial(_dwconv_kernel, width=width, rows_chunk=rows_chunk),
        grid=(bsz, length // bt, c // cw),
        in_specs=[halo(lambda b, i, j: (b, jnp.maximum(i * per_block - 1, 0), j)),
                  pl.BlockSpec((1, bt, cw), lambda b, i, j: (b, i, j)),
                  halo(lambda b, i, j: (b, jnp.minimum((i + 1) * per_block, n_halo - 1), j)),
                  pl.BlockSpec((width, cw), lambda b, i, j: (0, j)),
                  pl.BlockSpec((1, cw), lambda b, i, j: (0, j))],
        out_specs=pl.BlockSpec((1, bt, cw), lambda b, i, j: (b, i, j)),
        out_shape=jax.ShapeDtypeStruct((bsz, length, c), F32),
        scratch_shapes=[pltpu.VMEM((bt + 2 * CONV_HALO, cw), F32),
                        pltpu.VMEM((F32_SUBLANES - 1, bt + _conv_aligned_span(width), cw), F32)],
        compiler_params=_params(("arbitrary",) * 3,
                                _vmem_limit(4 * _nbytes((bt, cw), F32),
                                            scratch=F32_SUBLANES * _nbytes((bt + 2 * CONV_HALO, cw), F32))),
        name="depthwise_conv",
    )(u3, u3, u3, w_dw, b_dw.reshape(1, c))
    return out.reshape(m, c)


def kernel(x, c, ctx, c_ctx, ada_down, ada_up, ada_b, norm_w, ffn_w_in, ffn_w_out, fourier_w, fourier_b,
           attn_w_qkv, attn_q_norm, attn_k_norm, attn_w_o, conv_w_pw1, conv_b_pw1, conv_w_dw, conv_b_dw,
           conv_ln_w, conv_ln_b, conv_w_pw2, conv_b_pw2, final_norm_w):
    bsz, seq, d = x.shape
    ctx_len = ctx.shape[1]
    depth = ada_down.shape[0]
    assert bsz + 1 <= MOD_ROWS and d % (FOURIER_GROUPS * V7X_LANES) == 0 and seq % GRID_W == 0
    assert seq % (16 * DFT_SLABS) == 0 and ctx_len % (16 * DFT_SLABS) == 0

    cond = jnp.zeros((MOD_ROWS, d), F32).at[:bsz].set(c).at[bsz].set(c_ctx)
    mod = _ada_modulation(cond, ada_down, ada_up, ada_b)

    lat = _Groups(0, seq)
    con = _Groups(bsz, None)
    def norm_of(layer, s, groups):
        if layer >= depth:
            return None
        return _Norm(norm_w[layer, s], mod[layer][:, 3 * s], mod[layer][:, 3 * s + 1], groups)

    def ctx_in(layer):
        return layer < depth and (layer < depth - 1 or layer % N_MIXERS == 1)

    s_lat = _prep_stream(x.reshape(bsz * seq, d), norm_of(0, 0, lat))
    s_ctx = _prep_stream(ctx.reshape(bsz * ctx_len, d), norm_of(0, 0, con))
    rope = _rope_tables(seq)
    ffn_w_in, ffn_w_out, fourier_w, attn_w_qkv, attn_w_o, conv_w_pw1, conv_w_pw2 = (
        w.astype(BF16) for w in (ffn_w_in, ffn_w_out, fourier_w, attn_w_qkv, attn_w_o, conv_w_pw1, conv_w_pw2))

    for i in range(depth):
        kind = i % N_MIXERS
        j = i // N_MIXERS
        ctx_out = i < depth - 1
        m = mod[i]

        def ffn(s, which, nxt_lat, nxt_ctx, run_ctx):
            nonlocal s_lat, s_ctx
            w_in, w_out = _Weight(ffn_w_in, (i, which)), _Weight(ffn_w_out, (i, which))
            c = _shift_mm(norm_of(i, s, lat), w_in, None)
            gate = m[:, 3 * s + 2]
            act = _gated_mm(s_lat, lat, w_in, c, "swiglu", BF16)
            s_lat = _res_mm(act, w_out, None, s_lat, gate, FFN_RES_WEIGHT, lat, nxt_lat)
            if run_ctx:
                act = _gated_mm(s_ctx, con, w_in, c, "swiglu", BF16)
                s_ctx = _res_mm(act, w_out, None, s_ctx, gate, FFN_RES_WEIGHT, con, nxt_ctx)

        ffn(0, 0, norm_of(i, 1, lat), norm_of(i, 1, con), ctx_in(i))

        n_lat, n_ctx = norm_of(i, 1, lat), norm_of(i, 1, con)
        nx_lat, nx_ctx = norm_of(i, 2, lat), norm_of(i, 2, con)
        gate = m[:, 5]
        if kind == 0:
            w, b = _Weight(fourier_w, (j,)), fourier_b[j]
            y = _fourier_real_2d(s_lat, n_lat, bsz, seq)
            s_lat = _res_mm(y, w, b, s_lat, gate, 1.0, lat, nx_lat)
            if ctx_out:
                y = _fourier_real_2d(s_ctx, n_ctx, bsz, ctx_len)
                s_ctx = _res_mm(y, w, b, s_ctx, gate, 1.0, con, nx_ctx)
        elif kind == 1:
            w_qkv, w_o = _Weight(attn_w_qkv, (j,)), _Weight(attn_w_o, (j,))
            kvw = (w_qkv.shape[1] - d) // 2
            c = _shift_mm(n_lat, w_qkv, None)
            qkv_x = _qkv_project(s_lat, lat, w_qkv, c, attn_q_norm[j], attn_k_norm[j], rope, seq)
            qkv_c = _qkv_project(s_ctx, con, w_qkv, c, attn_q_norm[j], attn_k_norm[j], None, ctx_len)
            qkv_x = qkv_x.reshape(bsz, seq, -1)
            qkv_c = qkv_c.reshape(bsz, ctx_len, -1)
            o_x = _attention(qkv_x, [qkv_x, qkv_c], d, kvw).reshape(bsz * seq, d)
            s_lat = _res_mm(o_x, w_o, None, s_lat, gate, 1.0, lat, nx_lat)
            if ctx_out:
                o_c = _attention(qkv_c, [qkv_c], d, kvw).reshape(bsz * ctx_len, d)
                s_ctx = _res_mm(o_c, w_o, None, s_ctx, gate, 1.0, con, nx_ctx)
        else:
            w1, w2 = _Weight(conv_w_pw1, (j,)), _Weight(conv_w_pw2, (j,))
            c = _shift_mm(n_lat, w1, conv_b_pw1[j])

            def conv(stream, groups, length):
                u = _gated_mm(stream, groups, w1, c, "glu", F32)
                u = _depthwise_conv(u, conv_w_dw[j], conv_b_dw[j], bsz, length)
                return _ln_silu(u, conv_ln_w[j], conv_ln_b[j])

            s_lat = _res_mm(conv(s_lat, lat, seq), w2, conv_b_pw2[j], s_lat, gate, 1.0, lat, nx_lat)
            if ctx_out:
                s_ctx = _res_mm(conv(s_ctx, con, ctx_len), w2, conv_b_pw2[j], s_ctx, gate, 1.0, con, nx_ctx)

        ffn(2, 1, norm_of(i + 1, 0, lat), norm_of(i + 1, 0, con) if ctx_in(i + 1) else None, ctx_out)

    return _rms_norm(s_lat.x, final_norm_w).reshape(bsz, seq, d)
```

```python
import functools
import math
from typing import NamedTuple

import jax
import jax.numpy as jnp
import numpy as np
from jax import lax
from jax.experimental import pallas as pl
from jax.experimental.pallas import tpu as pltpu

F32 = jnp.float32
BF16 = jnp.bfloat16

HEAD_DIM = 128
GQA_GROUP = 4
GRID_W = 64
FOURIER_GROUPS = 4
N_MIXERS = 3
N_MOD = 9
FFN_RES_WEIGHT = 0.5
ROPE_THETA = 10000.0
EPS = 1e-6

V7X_VMEM_BYTES = 64 * 1024 * 1024
V7X_VMEM_REQUEST_CAP = V7X_VMEM_BYTES - 6 * 1024 * 1024
V7X_VMEM_REQUEST_MIN = 16 * 1024 * 1024
V7X_LANES = 128
F32_SUBLANES = 8
MOD_ROWS = 8
CONV_HALO = 16

MM_ROWS = 1024
MM_COLS = 512
ROW_TILE = 512
CONV_ROWS = 256


def _vmem_limit(*block_bytes, scratch=0):
    need = 2 * sum(block_bytes) + scratch
    return int(min(V7X_VMEM_REQUEST_CAP, max(2 * need, V7X_VMEM_REQUEST_MIN)))


def _params(sem, vmem):
    return pltpu.CompilerParams(dimension_semantics=sem, vmem_limit_bytes=vmem)


def _largest_block(n, cap, mult):
    if n <= cap:
        return n
    b = (cap // mult) * mult
    while b >= mult:
        if n % b == 0:
            return b
        b -= mult
    raise ValueError(f"no block for {n} (cap {cap}, multiple {mult})")


def _nbytes(shape, dtype):
    return int(np.prod(shape)) * jnp.dtype(dtype).itemsize


def _silu(x):
    return x * jax.nn.sigmoid(x)


def _ada_down_kernel(c_ref, w_ref, o_ref):
    s = _silu(c_ref[...])
    o_ref[0] = jnp.dot(s.astype(BF16), w_ref[0].astype(BF16), preferred_element_type=F32)


def _ada_up_kernel(h_ref, w_ref, b_ref, o_ref):
    o_ref[0] = jnp.dot(h_ref[0].astype(BF16), w_ref[0].astype(BF16),
                       preferred_element_type=F32) + b_ref[0]


def _ada_modulation(cond, down, up, bias):
    depth, d, rank = down.shape
    nout = up.shape[2]
    bn = _largest_block(rank, 256, V7X_LANES)
    h = pl.pallas_call(
        _ada_down_kernel,
        grid=(depth, rank // bn),
        in_specs=[pl.BlockSpec((MOD_ROWS, d), lambda l, j: (0, 0)),
                  pl.BlockSpec((1, d, bn), lambda l, j: (l, 0, j))],
        out_specs=pl.BlockSpec((1, MOD_ROWS, bn), lambda l, j: (l, 0, j)),
        out_shape=jax.ShapeDtypeStruct((depth, MOD_ROWS, rank), F32),
        compiler_params=_params(("arbitrary", "arbitrary"),
                                _vmem_limit(_nbytes((d, bn), F32), _nbytes((MOD_ROWS, d), F32))),
        name="ada_down",
    )(cond, down)
    bn = _largest_block(nout, 2048, V7X_LANES)
    m = pl.pallas_call(
        _ada_up_kernel,
        grid=(depth, nout // bn),
        in_specs=[pl.BlockSpec((1, MOD_ROWS, rank), lambda l, j: (l, 0, 0)),
                  pl.BlockSpec((1, rank, bn), lambda l, j: (l, 0, j)),
                  pl.BlockSpec((1, 1, bn), lambda l, j: (l, 0, j))],
        out_specs=pl.BlockSpec((1, MOD_ROWS, bn), lambda l, j: (l, 0, j)),
        out_shape=jax.ShapeDtypeStruct((depth, MOD_ROWS, nout), F32),
        compiler_params=_params(("arbitrary", "arbitrary"), _vmem_limit(_nbytes((rank, bn), F32))),
        name="ada_up",
    )(h, up, bias.reshape(depth, 1, nout))
    return m.reshape(depth, MOD_ROWS, N_MOD, d)


class _Groups:
    def __init__(self, first, rows_per_group):
        self.first = first
        self.rows_per_group = rows_per_group

    def index(self, i, bm):
        if self.rows_per_group is None:
            return self.first
        return self.first + (i * bm) // self.rows_per_group

    def batch_index(self, b):
        return self.first if self.rows_per_group is None else self.first + b

    def block_rows(self, m, cap):
        limit = m if self.rows_per_group is None else self.rows_per_group
        return _largest_block(limit, cap, F32_SUBLANES * 2)


class _Norm(NamedTuple):
    w: jax.Array
    shift: jax.Array
    scale: jax.Array
    groups: _Groups

    def gain_specs_and_args(self, bm, bn, col):
        r, d = self.scale.shape
        g = self.groups
        return ([pl.BlockSpec((1, bn), lambda *idx: (0, col(*idx))),
                 pl.BlockSpec((1, 1, bn), lambda *idx: (g.index(idx[0], bm), 0, col(*idx)))],
                [self.w.reshape(1, d), self.scale.reshape(r, 1, d)])


class _Stream(NamedTuple):
    x: jax.Array
    xg: jax.Array
    sq: jax.Array


def _lane_partial_sq(x):
    sq = x * x
    part = sq[:, :V7X_LANES]
    for t in range(1, x.shape[1] // V7X_LANES):
        part = part + sq[:, t * V7X_LANES:(t + 1) * V7X_LANES]
    return part


def _row_rstd(sq_ref, d):
    ms = jnp.sum(sq_ref[...], axis=-1, keepdims=True) / d
    return jnp.broadcast_to(lax.rsqrt(ms + EPS), sq_ref.shape)


def _scale_rows(acc, rstd):
    return jnp.concatenate([acc[:, t * V7X_LANES:(t + 1) * V7X_LANES] * rstd
                            for t in range(acc.shape[1] // V7X_LANES)], axis=1)


def _prep_kernel(x_ref, nw_ref, sc_ref, xg_ref, sq_ref):
    x = x_ref[...]
    xg_ref[...] = (x * (nw_ref[...] * (1.0 + sc_ref[0]))).astype(xg_ref.dtype)
    sq_ref[...] = _lane_partial_sq(x)


def _prep_stream(x, norm):
    m, d = x.shape
    bm = norm.groups.block_rows(m, ROW_TILE)
    row = lambda n: pl.BlockSpec((bm, n), lambda i: (i, 0))
    gspecs, gargs = norm.gain_specs_and_args(bm, d, lambda i: 0)
    xg, sq = pl.pallas_call(
        _prep_kernel,
        grid=(m // bm,),
        in_specs=[row(d)] + gspecs,
        out_specs=[row(d), row(V7X_LANES)],
        out_shape=[jax.ShapeDtypeStruct((m, d), BF16), jax.ShapeDtypeStruct((m, V7X_LANES), F32)],
        compiler_params=_params(("arbitrary",), _vmem_limit(_nbytes((bm, d), F32), _nbytes((bm, d), BF16))),
        name="prep_stream",
    )(x, *gargs)
    return _Stream(x, xg, sq)


def _rms_norm_kernel(x_ref, nw_ref, o_ref):
    x = x_ref[...]
    o_ref[...] = x * lax.rsqrt(jnp.mean(x * x, axis=-1, keepdims=True) + EPS) * nw_ref[...]


def _rms_norm(x, norm_w):
    m, d = x.shape
    bm = _largest_block(m, ROW_TILE, F32_SUBLANES)
    return pl.pallas_call(
        _rms_norm_kernel,
        grid=(m // bm,),
        in_specs=[pl.BlockSpec((bm, d), lambda i: (i, 0)), pl.BlockSpec((1, d), lambda i: (0, 0))],
        out_specs=pl.BlockSpec((bm, d), lambda i: (i, 0)),
        out_shape=jax.ShapeDtypeStruct((m, d), F32),
        compiler_params=_params(("arbitrary",), _vmem_limit(2 * _nbytes((bm, d), F32))),
        name="final_rms_norm",
    )(x, norm_w.reshape(1, d))


def _ln_silu_kernel(u_ref, w_ref, b_ref, o_ref):
    u = u_ref[...]
    xc = u - jnp.mean(u, axis=-1, keepdims=True)
    y = xc * lax.rsqrt(jnp.mean(xc * xc, axis=-1, keepdims=True) + EPS)
    o_ref[...] = _silu(y * w_ref[...] + b_ref[...]).astype(o_ref.dtype)


def _ln_silu(u, w, b):
    m, d = u.shape
    bm = _largest_block(m, ROW_TILE, F32_SUBLANES * 2)
    vec = pl.BlockSpec((1, d), lambda i: (0, 0))
    return pl.pallas_call(
        _ln_silu_kernel,
        grid=(m // bm,),
        in_specs=[pl.BlockSpec((bm, d), lambda i: (i, 0)), vec, vec],
        out_specs=pl.BlockSpec((bm, d), lambda i: (i, 0)),
        out_shape=jax.ShapeDtypeStruct((m, d), BF16),
        compiler_params=_params(("arbitrary",), _vmem_limit(_nbytes((bm, d), F32), _nbytes((bm, d), BF16))),
        name="ln_silu",
    )(u, w.reshape(1, d), b.reshape(1, d))


class _Weight(NamedTuple):
    array: jax.Array
    lead: tuple

    @property
    def shape(self):
        return self.array.shape[len(self.lead):]

    def spec(self, bn, col):
        lead = self.lead
        k = self.shape[0]
        return pl.BlockSpec((None,) * len(lead) + (k, bn), lambda *idx: (*lead, 0, col(*idx)))


def _shift_mm_kernel(*refs, has_bias):
    s_ref, w_ref = refs[:2]
    o_ref = refs[-1]
    y = jnp.dot(s_ref[...].astype(BF16), w_ref[...], preferred_element_type=F32)
    o_ref[...] = y + refs[2][...] if has_bias else y


def _shift_mm(norm, w, bias):
    r, k = norm.shift.shape
    n = w.shape[1]
    bn = _largest_block(n, 1024, V7X_LANES)
    in_specs = [pl.BlockSpec((r, k), lambda j: (0, 0)), w.spec(bn, lambda j: j)]
    args = [norm.shift, w.array]
    if bias is not None:
        in_specs.append(pl.BlockSpec((1, bn), lambda j: (0, j)))
        args.append(bias.reshape(1, n))
    out = pl.pallas_call(
        functools.partial(_shift_mm_kernel, has_bias=bias is not None),
        grid=(n // bn,),
        in_specs=in_specs,
        out_specs=pl.BlockSpec((r, bn), lambda j: (0, j)),
        out_shape=jax.ShapeDtypeStruct((r, n), F32),
        compiler_params=_params(("arbitrary",), _vmem_limit(_nbytes((k, bn), BF16), _nbytes((r, k), F32))),
        name="shift_mm",
    )(*args)
    return out.reshape(r, 1, n)


def _gated_mm_kernel(xg_ref, sq_ref, wa_ref, wb_ref, ca_ref, cb_ref, o_ref, rstd_sc, *, mode):
    @pl.when(pl.program_id(1) == 0)
    def _():
        rstd_sc[...] = _row_rstd(sq_ref, xg_ref.shape[1])

    a = xg_ref[...]
    rstd = rstd_sc[...]
    ga = _scale_rows(jnp.dot(a, wa_ref[...], preferred_element_type=F32), rstd) + ca_ref[0]
    gb = _scale_rows(jnp.dot(a, wb_ref[...], preferred_element_type=F32), rstd) + cb_ref[0]
    r = _silu(ga) * gb if mode == "swiglu" else ga * jax.nn.sigmoid(gb)
    o_ref[...] = r.astype(o_ref.dtype)


def _gated_mm(stream, groups, w, c, mode, out_dtype):
    m, k = stream.xg.shape
    f = w.shape[1] // 2
    bm = groups.block_rows(m, MM_ROWS)
    bn = _largest_block(f, MM_COLS, V7X_LANES)
    nb = f // bn
    g = groups
    add = lambda off: pl.BlockSpec((1, 1, bn), lambda i, j: (g.index(i, bm), 0, j + off))
    return pl.pallas_call(
        functools.partial(_gated_mm_kernel, mode=mode),
        grid=(m // bm, nb),
        in_specs=[pl.BlockSpec((bm, k), lambda i, j: (i, 0)),
                  pl.BlockSpec((bm, V7X_LANES), lambda i, j: (i, 0)),
                  w.spec(bn, lambda i, j: j), w.spec(bn, lambda i, j: j + nb), add(0), add(nb)],
        out_specs=pl.BlockSpec((bm, bn), lambda i, j: (i, j)),
        out_shape=jax.ShapeDtypeStruct((m, f), out_dtype),
        scratch_shapes=[pltpu.VMEM((bm, V7X_LANES), F32)],
        compiler_params=_params(("arbitrary", "arbitrary"),
                                _vmem_limit(_nbytes((bm, k), BF16), 2 * _nbytes((k, bn), BF16),
                                            _nbytes((bm, bn), out_dtype), scratch=4 * _nbytes((bm, bn), F32))),
        name=f"gated_mm_{mode}",
    )(stream.xg, stream.sq, w.array, w.array, c, c)


def _res_mm_kernel(*refs, coef, has_bias, emit_stream):
    a_ref, w_ref = refs[:2]
    n_in = 5 if has_bias else 4
    x_ref, g_ref = refs[n_in - 2:n_in]
    y = jnp.dot(a_ref[...], w_ref[...], preferred_element_type=F32)
    if has_bias:
        y = y + refs[2][...]
    g = g_ref[0] if coef == 1.0 else coef * g_ref[0]
    xn = x_ref[...] + g * y
    if not emit_stream:
        refs[n_in][...] = xn
        return
    nw_ref, sc_ref, o_ref, xg_ref, sq_ref = refs[n_in:]
    o_ref[...] = xn
    xg_ref[...] = (xn * (nw_ref[...] * (1.0 + sc_ref[0]))).astype(xg_ref.dtype)
    part = _lane_partial_sq(xn)
    j = pl.program_id(1)

    @pl.when(j == 0)
    def _():
        sq_ref[...] = part

    @pl.when(j > 0)
    def _():
        sq_ref[...] += part


def _res_mm(a, w, bias, stream, gate, coef, groups, next_norm):
    x = stream.x
    m, k = a.shape
    n = w.shape[1]
    bm = groups.block_rows(m, MM_ROWS)
    bn = _largest_block(n, MM_COLS, V7X_LANES)
    r = gate.shape[0]
    in_specs = [pl.BlockSpec((bm, k), lambda i, j: (i, 0)), w.spec(bn, lambda i, j: j)]
    args = [a, w.array]
    if bias is not None:
        in_specs.append(pl.BlockSpec((1, bn), lambda i, j: (0, j)))
        args.append(bias.reshape(1, n))
    tile = pl.BlockSpec((bm, bn), lambda i, j: (i, j))
    in_specs += [tile, pl.BlockSpec((1, 1, bn), lambda i, j: (groups.index(i, bm), 0, j))]
    args += [x, gate.reshape(r, 1, n)]
    out_specs = [tile]
    out_shape = [jax.ShapeDtypeStruct((m, n), F32)]
    if next_norm is not None:
        gspecs, gargs = next_norm.gain_specs_and_args(bm, bn, lambda i, j: j)
        in_specs += gspecs
        args += gargs
        out_specs += [tile, pl.BlockSpec((bm, V7X_LANES), lambda i, j: (i, 0))]
        out_shape += [jax.ShapeDtypeStruct((m, n), BF16), jax.ShapeDtypeStruct((m, V7X_LANES), F32)]
    out = pl.pallas_call(
        functools.partial(_res_mm_kernel, coef=coef, has_bias=bias is not None, emit_stream=next_norm is not None),
        grid=(m // bm, n // bn),
        in_specs=in_specs,
        out_specs=out_specs,
        out_shape=out_shape,
        compiler_params=_params(("arbitrary", "arbitrary"),
                                _vmem_limit(_nbytes((bm, k), BF16), _nbytes((k, bn), BF16),
                                            3 * _nbytes((bm, bn), F32), scratch=_nbytes((bm, bn), F32))),
        name="res_mm",
    )(*args)
    return _Stream(*out) if next_norm is not None else _Stream(out[0], None, None)


def _rope_tables(seq_len):
    n_rows = seq_len // GRID_W
    pairs = HEAD_DIM // 4
    row = jnp.broadcast_to(jnp.arange(n_rows, dtype=F32)[:, None], (n_rows, GRID_W)).reshape(-1)
    col = jnp.broadcast_to(jnp.arange(GRID_W, dtype=F32)[None, :], (n_rows, GRID_W)).reshape(-1)
    freqs = ROPE_THETA ** (-jnp.arange(pairs, dtype=F32) / pairs)
    ang = jnp.stack([row[:, None] * freqs, col[:, None] * freqs], axis=1)
    cos = jnp.cos(ang)
    sin = jnp.sin(ang)
    cos_t = jnp.stack([cos, cos], axis=2).reshape(seq_len, HEAD_DIM)
    sin_t = jnp.stack([-sin, sin], axis=2).reshape(seq_len, HEAD_DIM)
    return cos_t, sin_t


def _qkv_kernel(*refs, n_norm_blocks, rope):
    xg_ref, sq_ref, w_ref, c_ref, nw_ref = refs[:5]
    if rope:
        cos_ref, sin_ref = refs[5:7]
    o_ref, rstd_sc = refs[-2:]
    j = pl.program_id(1)

    @pl.when(j == 0)
    def _():
        rstd_sc[...] = _row_rstd(sq_ref, xg_ref.shape[1])

    acc = _scale_rows(jnp.dot(xg_ref[...], w_ref[...], preferred_element_type=F32), rstd_sc[...]) + c_ref[0]
    bn = acc.shape[1]
    pairs = HEAD_DIM // 4

    @pl.when(j < n_norm_blocks)
    def _():
        if rope:
            cos = cos_ref[...]
            sin = sin_ref[...]
            lane = lax.broadcasted_iota(jnp.int32, (1, HEAD_DIM), 1)
            first_half = (lane % (2 * pairs)) < pairs
        for h in range(bn // HEAD_DIM):
            sl = slice(h * HEAD_DIM, (h + 1) * HEAD_DIM)
            t = acc[:, sl]
            y = t * lax.rsqrt(jnp.mean(t * t, axis=-1, keepdims=True) + EPS) * nw_ref[:, sl]
            if rope:
                partner = jnp.where(first_half, pltpu.roll(y, HEAD_DIM - pairs, 1), pltpu.roll(y, pairs, 1))
                y = y * cos + partner * sin
            o_ref[:, sl] = y.astype(o_ref.dtype)

    @pl.when(j >= n_norm_blocks)
    def _():
        o_ref[...] = acc.astype(o_ref.dtype)


def _qkv_project(stream, groups, w_qkv, c, q_norm, k_norm, rope_tables, seq_len):
    m, d = stream.xg.shape
    n = w_qkv.shape[1]
    kv_width = (n - d) // 2
    bm = _largest_block(seq_len, MM_ROWS, 16)
    bn = _largest_block(math.gcd(d, kv_width), MM_COLS, HEAD_DIM)
    q_scale = HEAD_DIM ** -0.5 * math.log2(math.e)
    nw = jnp.concatenate([jnp.tile(q_norm * q_scale, d // HEAD_DIM), jnp.tile(k_norm, kv_width // HEAD_DIM),
                          jnp.ones((kv_width,), F32)]).reshape(1, n)
    g = groups
    in_specs = [pl.BlockSpec((bm, d), lambda i, j: (i, 0)),
                pl.BlockSpec((bm, V7X_LANES), lambda i, j: (i, 0)),
                w_qkv.spec(bn, lambda i, j: j),
                pl.BlockSpec((1, 1, bn), lambda i, j: (g.index(i, bm), 0, j)),
                pl.BlockSpec((1, bn), lambda i, j: (0, j))]
    args = [stream.xg, stream.sq, w_qkv.array, c, nw]
    if rope_tables is not None:
        pos_blocks = seq_len // bm
        pos_spec = pl.BlockSpec((bm, HEAD_DIM), lambda i, j: (i % pos_blocks, 0))
        in_specs += [pos_spec, pos_spec]
        args += list(rope_tables)
    return pl.pallas_call(
        functools.partial(_qkv_kernel, n_norm_blocks=(d + kv_width) // bn, rope=rope_tables is not None),
        grid=(m // bm, n // bn),
        in_specs=in_specs,
        out_specs=pl.BlockSpec((bm, bn), lambda i, j: (i, j)),
        out_shape=jax.ShapeDtypeStruct((m, n), BF16),
        scratch_shapes=[pltpu.VMEM((bm, V7X_LANES), F32)],
        compiler_params=_params(("arbitrary", "arbitrary"),
                                _vmem_limit(_nbytes((bm, d), BF16), _nbytes((d, bn), BF16), _nbytes((bm, bn), BF16),
                                            scratch=3 * _nbytes((bm, bn), F32))),
        name="qkv_project",
    )(*args)


ATTN_KEY_CHUNK = 1024
ATTN_Q_ROWS = 128
ATTN_MAX_PARTS = 4


def _key_chunks(lengths):
    chunks, col = [], 0
    for src, length in enumerate(lengths):
        size = _largest_block(length, ATTN_KEY_CHUNK, V7X_LANES)
        for start in range(0, length, size):
            chunks.append((src, start, size, col))
            col += size
    return chunks


def _attn_kernel(*refs, n_src):
    q_ref, qn_ref = refs[:2]
    k_refs = refs[2:2 + n_src]
    v_refs = refs[2 + n_src:2 + 2 * n_src]
    o_ref, s_sc, m_sc = refs[2 + 2 * n_src:]
    bq = ATTN_Q_ROWS
    rows = GQA_GROUP * bq
    n_parts = q_ref.shape[1] // bq
    chunks = _key_chunks([k.shape[1] for k in k_refs])

    def q_rows(ref, part):
        return jnp.concatenate([ref[0, part * bq:(part + 1) * bq, g * HEAD_DIM:(g + 1) * HEAD_DIM]
                                for g in range(GQA_GROUP)], axis=0)

    def step(cur, m8_cur, q):
        nxt = 0 if cur is None else cur + 1
        m8 = None
        if cur is not None:
            l8 = jnp.zeros((F32_SUBLANES, rows), F32)
            acc = jnp.zeros((HEAD_DIM, rows), F32)
        for src, start, size, col in chunks:
            st = lax.dot_general(k_refs[src][0, start:start + size, :], q, (((1,), (1,)), ((), ())),
                                 preferred_element_type=F32)
            s_sc[nxt % 2, col:col + size, :] = st
            cm = jnp.max(st.reshape(size // F32_SUBLANES, F32_SUBLANES, rows), axis=0)
            m8 = cm if m8 is None else jnp.maximum(m8, cm)
            if cur is not None:
                s = s_sc[cur % 2, col:col + size, :].reshape(size // F32_SUBLANES, F32_SUBLANES, rows)
                p = jnp.exp2(s - m8_cur[None])
                l8 = l8 + jnp.sum(p, axis=0)
                pt = p.reshape(size, rows).astype(BF16)
                acc = acc + lax.dot_general(v_refs[src][0, start:start + size, :], pt, (((0,), (0,)), ((), ())),
                                            preferred_element_type=F32)
        if cur is not None:
            out_t = acc / jnp.sum(l8, axis=0, keepdims=True)
            for g in range(GQA_GROUP):
                o_ref[0, cur * bq:(cur + 1) * bq, g * HEAD_DIM:(g + 1) * HEAD_DIM] = (
                    out_t[:, g * bq:(g + 1) * bq].T.astype(o_ref.dtype))
        return jnp.broadcast_to(jnp.max(m8, axis=0, keepdims=True), (F32_SUBLANES, rows))

    @pl.when(pl.program_id(2) == 0)
    def _():
        m_sc[...] = step(None, None, q_rows(q_ref, 0))

    m8 = m_sc[...]
    for part in range(n_parts):
        q_next = q_rows(q_ref, part + 1) if part + 1 < n_parts else q_rows(qn_ref, 0)
        m8 = step(part, m8, q_next)
    m_sc[...] = m8


def _attention(q_src, kv_srcs, d, kv_width):
    b, lq, _ = q_src.shape
    n_kv = kv_width // HEAD_DIM
    gw = GQA_GROUP * HEAD_DIM
    assert lq % ATTN_Q_ROWS == 0
    bq = _largest_block(lq, ATTN_Q_ROWS * ATTN_MAX_PARTS, ATTN_Q_ROWS)
    rows = GQA_GROUP * ATTN_Q_ROWS
    k0 = d // HEAD_DIM
    v0 = (d + kv_width) // HEAD_DIM
    n_keys = sum(a.shape[1] for a in kv_srcs)
    n_parts = bq // ATTN_Q_ROWS
    assert n_parts % 2 == 0
    last_part = lq // ATTN_Q_ROWS - 1
    q_spec = pl.BlockSpec((1, bq, gw), lambda bi, hi, qi: (bi, qi, hi))
    qn_spec = pl.BlockSpec((1, ATTN_Q_ROWS, gw),
                           lambda bi, hi, qi: (bi, jnp.minimum((qi + 1) * n_parts, last_part), hi))
    k_specs = [pl.BlockSpec((1, a.shape[1], HEAD_DIM), lambda bi, hi, qi: (bi, 0, k0 + hi)) for a in kv_srcs]
    v_specs = [pl.BlockSpec((1, a.shape[1], HEAD_DIM), lambda bi, hi, qi: (bi, 0, v0 + hi)) for a in kv_srcs]
    return pl.pallas_call(
        functools.partial(_attn_kernel, n_src=len(kv_srcs)),
        grid=(b, n_kv, lq // bq),
        in_specs=[q_spec, qn_spec] + k_specs + v_specs,
        out_specs=pl.BlockSpec((1, bq, gw), lambda bi, hi, qi: (bi, qi, hi)),
        out_shape=jax.ShapeDtypeStruct((b, lq, d), BF16),
        scratch_shapes=[pltpu.VMEM((2, n_keys, rows), F32), pltpu.VMEM((F32_SUBLANES, rows), F32)],
        compiler_params=_params(("arbitrary", "arbitrary", "arbitrary"),
                                _vmem_limit(2 * _nbytes((bq, gw), BF16), 2 * _nbytes((n_keys, HEAD_DIM), BF16),
                                            scratch=2 * _nbytes((rows, n_keys), F32)
                                            + 2 * _nbytes((rows, ATTN_KEY_CHUNK), F32))),
        name="attention",
    )(q_src, q_src, *kv_srcs, *kv_srcs)


DFT_SLABS = 8


def _dft_tables(n):
    idx = np.arange(n, dtype=np.int64)
    ang = 2.0 * np.pi * ((idx[:, None] * idx[None, :]) % n) / n
    return np.cos(ang), np.sin(ang)


def _lincomb(coefs, xs):
    groups = {}
    for cf, x in zip(coefs, xs):
        mag = round(abs(float(cf)), 12)
        if mag == 0.0:
            continue
        pos, neg = groups.setdefault(mag, ([], []))
        (pos if cf > 0 else neg).append(x)
    total = None
    for mag, (pos, neg) in groups.items():
        term = functools.reduce(lambda a, b: a + b, pos) if pos else None
        if neg:
            nsum = functools.reduce(lambda a, b: a + b, neg)
            term = -nsum if term is None else term - nsum
        if mag != 1.0:
            term = term * mag
        total = term if total is None else total + term
    return total


def _dft_slab_kernel(xg_ref, sq_ref, sh_ref, tc_ref, ts_ref, re_ref, im_ref, *, d_model):
    cos, sin = _dft_tables(DFT_SLABS)
    rstd = []
    for n2 in range(DFT_SLABS):
        r = lax.rsqrt(jnp.sum(sq_ref[0, n2], axis=-1, keepdims=True) / d_model + EPS)
        rstd.append(jnp.broadcast_to(r, sq_ref.shape[2:]))
    for lb in range(xg_ref.shape[3] // V7X_LANES):
        sl = slice(lb * V7X_LANES, (lb + 1) * V7X_LANES)
        sh = sh_ref[0, :, sl]
        xs = [xg_ref[0, n2, :, sl].astype(F32) * rstd[n2] + sh for n2 in range(DFT_SLABS)]
        uv = {}
        for k2 in range(DFT_SLABS):
            tc = tc_ref[k2]
            ts = ts_ref[k2]
            mirror = DFT_SLABS - k2
            if mirror in uv:
                u, v = uv[mirror]
                re, im = u * tc + v * ts, v * tc - u * ts
            else:
                u, v = _lincomb(cos[k2], xs), _lincomb(sin[k2], xs)
                uv[k2] = (u, v)
                if v is None:
                    re, im = u * tc, -(u * ts)
                else:
                    re, im = u * tc - v * ts, -(v * tc) - u * ts
            re_ref[0, k2, :, sl] = re.astype(re_ref.dtype)
            im_ref[0, k2, :, sl] = im.astype(im_ref.dtype)


def _dft_dense_kernel(re_ref, im_ref, cs_ref, snc_ref, p_ref, q_ref):
    t = jnp.concatenate([re_ref[0, 0], im_ref[0, 0]], axis=0)
    p_ref[0, 0] = jnp.dot(cs_ref[...], t, preferred_element_type=F32).astype(p_ref.dtype)
    q_ref[0, 0] = jnp.dot(snc_ref[...], t, preferred_element_type=F32).astype(q_ref.dtype)


def _chan_dft_kernel(p_ref, q_ref, csn_ref, o_ref):
    pq = jnp.concatenate([p_ref[...], q_ref[...]], axis=1)
    o_ref[...] = jnp.dot(pq, csn_ref[...], preferred_element_type=F32).astype(o_ref.dtype)


def _fourier_real_2d(stream, norm, bsz, length):
    m, d = stream.xg.shape
    l2 = DFT_SLABS
    l1 = length // l2
    cg = d // FOURIER_GROUPS
    c1, s1 = _dft_tables(l1)
    cs1 = jnp.asarray(np.concatenate([c1, s1], axis=1), BF16)
    snc1 = jnp.asarray(np.concatenate([s1, -c1], axis=1), BF16)
    k2 = np.arange(l2, dtype=np.int64)[:, None]
    n1 = np.arange(l1, dtype=np.int64)[None, :]
    tw = 2.0 * np.pi * ((n1 * k2) % length) / length
    lanes = np.ones((1, 1, V7X_LANES))
    tc = jnp.asarray(np.cos(tw)[:, :, None] * lanes, F32)
    ts = jnp.asarray(np.sin(tw)[:, :, None] * lanes, F32)
    ortho = 1.0 / math.sqrt(length * cg)
    cc, sc = _dft_tables(cg)
    csn = jnp.asarray(np.concatenate([cc, -sc], axis=0) * ortho, BF16)

    rt = _largest_block(l1, 32, 16)
    ct = _largest_block(d, 1024, V7X_LANES)
    slab_spec = pl.BlockSpec((1, l2, rt, ct), lambda b, i, j: (b, 0, i, j))
    sq_spec = pl.BlockSpec((1, l2, rt, V7X_LANES), lambda b, i, j: (b, 0, i, 0))
    tw_spec = pl.BlockSpec((l2, rt, V7X_LANES), lambda b, i, j: (0, i, 0))
    r = norm.shift.shape[0]
    g = norm.groups
    sh_spec = pl.BlockSpec((1, 1, ct), lambda b, i, j: (g.batch_index(b), 0, j))
    slab_shape = jax.ShapeDtypeStruct((bsz, l2, l1, d), BF16)
    t_re, t_im = pl.pallas_call(
        functools.partial(_dft_slab_kernel, d_model=d),
        grid=(bsz, l1 // rt, d // ct),
        in_specs=[slab_spec, sq_spec, sh_spec, tw_spec, tw_spec],
        out_specs=[slab_spec, slab_spec],
        out_shape=[slab_shape, slab_shape],
        compiler_params=_params(("arbitrary",) * 3, _vmem_limit(3 * _nbytes((l2, rt, ct), BF16),
                                                                 3 * _nbytes((l2, rt, V7X_LANES), F32))),
        name="dft_pos_slabs",
    )(stream.xg.reshape(bsz, l2, l1, d), stream.sq.reshape(bsz, l2, l1, V7X_LANES),
      norm.shift.reshape(r, 1, d), tc, ts)

    ct2 = _largest_block(d, MM_COLS, V7X_LANES)
    mat1 = pl.BlockSpec((l1, 2 * l1), lambda b, s, j: (0, 0))
    blk = pl.BlockSpec((1, 1, l1, ct2), lambda b, s, j: (b, s, 0, j))
    p, q = pl.pallas_call(
        _dft_dense_kernel,
        grid=(bsz, l2, d // ct2),
        in_specs=[blk, blk, mat1, mat1],
        out_specs=[blk, blk],
        out_shape=[slab_shape, slab_shape],
        compiler_params=_params(("arbitrary",) * 3, _vmem_limit(4 * _nbytes((l1, ct2), BF16),
                                                                 2 * _nbytes((l1, 2 * l1), BF16),
                                                                 scratch=4 * _nbytes((l1, ct2), F32))),
        name="dft_pos_dense",
    )(t_re, t_im, cs1, snc1)

    bm = _largest_block(m, MM_ROWS, 16)
    pq_spec = pl.BlockSpec((bm, cg), lambda i, g: (i, g))
    mat_c = pl.BlockSpec((2 * cg, cg), lambda i, g: (0, 0))
    y = pl.pallas_call(
        _chan_dft_kernel,
        grid=(m // bm, FOURIER_GROUPS),
        in_specs=[pq_spec, pq_spec, mat_c],
        out_specs=pq_spec,
        out_shape=jax.ShapeDtypeStruct((m, d), BF16),
        compiler_params=_params(("arbitrary",) * 2, _vmem_limit(3 * _nbytes((bm, cg), BF16),
                                                                 2 * _nbytes((cg, cg), BF16),
                                                                 scratch=2 * _nbytes((bm, cg), F32))),
        name="dft_channel",
    )(p.reshape(m, d), q.reshape(m, d), csn)
    return y.reshape(bsz, l2, l1, d).transpose(0, 2, 1, 3).reshape(m, d)


def _conv_aligned_span(width):
    first = CONV_HALO - (width - 1) // 2
    return (first + width - 1) // F32_SUBLANES * F32_SUBLANES


def _dwconv_kernel(prev_ref, cur_ref, next_ref, w_ref, b_ref, o_ref, win_sc, sh_sc, *, width, rows_chunk):
    i = pl.program_id(1)
    bt = cur_ref.shape[1]
    first = CONV_HALO - (width - 1) // 2
    win_sc[0:CONV_HALO] = jnp.where(i > 0, prev_ref[0], 0.0)
    win_sc[CONV_HALO:CONV_HALO + bt] = cur_ref[0]
    win_sc[CONV_HALO + bt:] = jnp.where(i < pl.num_programs(1) - 1, next_ref[0], 0.0)
    span = bt + _conv_aligned_span(width)
    for r in range(1, F32_SUBLANES):
        sh_sc[r - 1] = win_sc[r:r + span]
    for c in range(bt // rows_chunk):
        acc = None
        for k in range(width):
            r = (first + k) % F32_SUBLANES
            a = first + k - r + c * rows_chunk
            rows = win_sc[a:a + rows_chunk] if r == 0 else sh_sc[r - 1, a:a + rows_chunk]
            term = rows * w_ref[k:k + 1]
            acc = term if acc is None else acc + term
        o_ref[0, c * rows_chunk:(c + 1) * rows_chunk] = acc + b_ref[...]


def _depthwise_conv(u, w_dw, b_dw, bsz, length):
    m, c = u.shape
    width = w_dw.shape[0]
    assert (width - 1) // 2 <= CONV_HALO
    assert _conv_aligned_span(width) + F32_SUBLANES <= 2 * CONV_HALO
    bt = _largest_block(length, CONV_ROWS, CONV_HALO)
    cw = _largest_block(c, 512, V7X_LANES)
    rows_chunk = _largest_block(bt, 32, F32_SUBLANES)
    per_block = bt // CONV_HALO
    n_halo = length // CONV_HALO
    u3 = u.reshape(bsz, length, c)
    halo = lambda f: pl.BlockSpec((1, CONV_HALO, cw), f)
    out = pl.pallas_call(
        functools.partial(_dwconv_kernel, width=width, rows_chunk=rows_chunk),
        grid=(bsz, length // bt, c // cw),
        in_specs=[halo(lambda b, i, j: (b, jnp.maximum(i * per_block - 1, 0), j)),
                  pl.BlockSpec((1, bt, cw), lambda b, i, j: (b, i, j)),
                  halo(lambda b, i, j: (b, jnp.minimum((i + 1) * per_block, n_halo - 1), j)),
                  pl.BlockSpec((width, cw), lambda b, i, j: (0, j)),
                  pl.BlockSpec((1, cw), lambda b, i, j: (0, j))],
        out_specs=pl.BlockSpec((1, bt, cw), lambda b, i, j: (b, i, j)),
        out_shape=jax.ShapeDtypeStruct((bsz, length, c), F32),
        scratch_shapes=[pltpu.VMEM((bt + 2 * CONV_HALO, cw), F32),
                        pltpu.VMEM((F32_SUBLANES - 1, bt + _conv_aligned_span(width), cw), F32)],
        compiler_params=_params(("arbitrary",) * 3,
                                _vmem_limit(4 * _nbytes((bt, cw), F32),
                                            scratch=F32_SUBLANES * _nbytes((bt + 2 * CONV_HALO, cw), F32))),
        name="depthwise_conv",
    )(u3, u3, u3, w_dw, b_dw.reshape(1, c))
    return out.reshape(m, c)


def kernel(x, c, ctx, c_ctx, ada_down, ada_up, ada_b, norm_w, ffn_w_in, ffn_w_out, fourier_w, fourier_b,
           attn_w_qkv, attn_q_norm, attn_k_norm, attn_w_o, conv_w_pw1, conv_b_pw1, conv_w_dw, conv_b_dw,
           conv_ln_w, conv_ln_b, conv_w_pw2, conv_b_pw2, final_norm_w):
    bsz, seq, d = x.shape
    ctx_len = ctx.shape[1]
    depth = ada_down.shape[0]
    assert bsz + 1 <= MOD_ROWS and d % (FOURIER_GROUPS * V7X_LANES) == 0 and seq % GRID_W == 0
    assert seq % (16 * DFT_SLABS) == 0 and ctx_len % (16 * DFT_SLABS) == 0

    cond = jnp.zeros((MOD_ROWS, d), F32).at[:bsz].set(c).at[bsz].set(c_ctx)
    mod = _ada_modulation(cond, ada_down, ada_up, ada_b)

    lat = _Groups(0, seq)
    con = _Groups(bsz, None)
    def norm_of(layer, s, groups):
        if layer >= depth:
            return None
        return _Norm(norm_w[layer, s], mod[layer][:, 3 * s], mod[layer][:, 3 * s + 1], groups)

    def ctx_in(layer):
        return layer < depth and (layer < depth - 1 or layer % N_MIXERS == 1)

    s_lat = _prep_stream(x.reshape(bsz * seq, d), norm_of(0, 0, lat))
    s_ctx = _prep_stream(ctx.reshape(bsz * ctx_len, d), norm_of(0, 0, con))
    rope = _rope_tables(seq)
    ffn_w_in, ffn_w_out, fourier_w, attn_w_qkv, attn_w_o, conv_w_pw1, conv_w_pw2 = (
        w.astype(BF16) for w in (ffn_w_in, ffn_w_out, fourier_w, attn_w_qkv, attn_w_o, conv_w_pw1, conv_w_pw2))

    for i in range(depth):
        kind = i % N_MIXERS
        j = i // N_MIXERS
        ctx_out = i < depth - 1
        m = mod[i]

        def ffn(s, which, nxt_lat, nxt_ctx, run_ctx):
            nonlocal s_lat, s_ctx
            w_in, w_out = _Weight(ffn_w_in, (i, which)), _Weight(ffn_w_out, (i, which))
            c = _shift_mm(norm_of(i, s, lat), w_in, None)
            gate = m[:, 3 * s + 2]
            act = _gated_mm(s_lat, lat, w_in, c, "swiglu", BF16)
            s_lat = _res_mm(act, w_out, None, s_lat, gate, FFN_RES_WEIGHT, lat, nxt_lat)
            if run_ctx:
                act = _gated_mm(s_ctx, con, w_in, c, "swiglu", BF16)
                s_ctx = _res_mm(act, w_out, None, s_ctx, gate, FFN_RES_WEIGHT, con, nxt_ctx)

        ffn(0, 0, norm_of(i, 1, lat), norm_of(i, 1, con), ctx_in(i))

        n_lat, n_ctx = norm_of(i, 1, lat), norm_of(i, 1, con)
        nx_lat, nx_ctx = norm_of(i, 2, lat), norm_of(i, 2, con)
        gate = m[:, 5]
        if kind == 0:
            w, b = _Weight(fourier_w, (j,)), fourier_b[j]
            y = _fourier_real_2d(s_lat, n_lat, bsz, seq)
            s_lat = _res_mm(y, w, b, s_lat, gate, 1.0, lat, nx_lat)
            if ctx_out:
                y = _fourier_real_2d(s_ctx, n_ctx, bsz, ctx_len)
                s_ctx = _res_mm(y, w, b, s_ctx, gate, 1.0, con, nx_ctx)
        elif kind == 1:
            w_qkv, w_o = _Weight(attn_w_qkv, (j,)), _Weight(attn_w_o, (j,))
            kvw = (w_qkv.shape[1] - d) // 2
            c = _shift_mm(n_lat, w_qkv, None)
            qkv_x = _qkv_project(s_lat, lat, w_qkv, c, attn_q_norm[j], attn_k_norm[j], rope, seq)
            qkv_c = _qkv_project(s_ctx, con, w_qkv, c, attn_q_norm[j], attn_k_norm[j], None, ctx_len)
            qkv_x = qkv_x.reshape(bsz, seq, -1)
            qkv_c = qkv_c.reshape(bsz, ctx_len, -1)
            o_x = _attention(qkv_x, [qkv_x, qkv_c], d, kvw).reshape(bsz * seq, d)
            s_lat = _res_mm(o_x, w_o, None, s_lat, gate, 1.0, lat, nx_lat)
            if ctx_out:
                o_c = _attention(qkv_c, [qkv_c], d, kvw).reshape(bsz * ctx_len, d)
                s_ctx = _res_mm(o_c, w_o, None, s_ctx, gate, 1.0, con, nx_ctx)
        else:
            w1, w2 = _Weight(conv_w_pw1, (j,)), _Weight(conv_w_pw2, (j,))
            c = _shift_mm(n_lat, w1, conv_b_pw1[j])

            def conv(stream, groups, length):
                u = _gated_mm(stream, groups, w1, c, "glu", F32)
                u = _depthwise_conv(u, conv_w_dw[j], conv_b_dw[j], bsz, length)
                return _ln_silu(u, conv_ln_w[j], conv_ln_b[j])

            s_lat = _res_mm(conv(s_lat, lat, seq), w2, conv_b_pw2[j], s_lat, gate, 1.0, lat, nx_lat)
            if ctx_out:
                s_ctx = _res_mm(conv(s_ctx, con, ctx_len), w2, conv_b_pw2[j], s_ctx, gate, 1.0, con, nx_ctx)

        ffn(2, 1, norm_of(i + 1, 0, lat), norm_of(i + 1, 0, con) if ctx_in(i + 1) else None, ctx_out)

    return _rms_norm(s_lat.x, final_norm_w).reshape(bsz, seq, d)
```
